```python
import math
import jax, jax.numpy as jnp
from jax import lax
import numpy as np

D_MODEL = 4096
BATCH = 1
SEQ = 8192
DEPTH = 1
DEC_BATCH = 2
DEC_SEQ = 4096
PAST_LEN = 128

HEAD_DIM = 128
N_Q_HEADS = 16
N_KV_HEADS = 4
ATTN_WIDTH = N_Q_HEADS * HEAD_DIM
KV_WIDTH = N_KV_HEADS * HEAD_DIM
WINDOW = 128
BLOCK = 128
ROPE_DIM = HEAD_DIM // 4
ROPE_THETA = 500000.0
SSM_WIDTH = 2048
SSM_GROUP = 16
SSM_GROUPS = SSM_WIDTH // SSM_GROUP
SSM_STATE = 64
N_DIR = 2
PEER_HEADS = 8
PEER_KEYS = 128
PEER_EXPERTS = PEER_KEYS * PEER_KEYS
PEER_TOPK = 16
PEER_QDIM = 256
PEER_HALF = PEER_QDIM // 2
PEER_CHUNK = 128
IN_WIDTH = ATTN_WIDTH + 2 * KV_WIDTH + SSM_WIDTH + 2 * D_MODEL
EPS = 1e-6
NEG = -1e30

kernel_name = 'hybrid_swa_s5_peer_encoder'


def rms_norm(x, g):
    xf = x.astype(jnp.float32)
    y = xf * lax.rsqrt(jnp.mean(xf * xf, axis=-1, keepdims=True) + EPS)
    return (y * g.astype(jnp.float32)).astype(x.dtype)


def partial_rope(x, pos):
    half = ROPE_DIM // 2
    inv = ROPE_THETA ** (-jnp.arange(half, dtype=jnp.float32) / half)
    ang = pos.astype(jnp.float32)[:, None] * inv[None, :]
    cos = jnp.cos(ang)[None, :, None, :]
    sin = jnp.sin(ang)[None, :, None, :]
    xr = x[..., :ROPE_DIM].astype(jnp.float32)
    x1, x2 = xr[..., :half], xr[..., half:]
    rot = jnp.concatenate([x1 * cos - x2 * sin, x2 * cos + x1 * sin], axis=-1)
    return jnp.concatenate([rot.astype(x.dtype), x[..., ROPE_DIM:]], axis=-1)


def window_attention(q, k, v, sink):
    B, L = q.shape[0], q.shape[1]
    nb = L // BLOCK
    G = N_Q_HEADS // N_KV_HEADS
    qb = q.reshape(B, nb, BLOCK, N_KV_HEADS, G, HEAD_DIM)
    pad = ((0, 0), (BLOCK, BLOCK), (0, 0), (0, 0))

    def neighbourhood(t):
        tb = jnp.pad(t, pad).reshape(B, nb + 2, BLOCK, N_KV_HEADS, HEAD_DIM)
        return jnp.concatenate([tb[:, :-2], tb[:, 1:-1], tb[:, 2:]], axis=2)

    kn, vn = neighbourhood(k), neighbourhood(v)
    s = jnp.einsum('bnqhgd,bnkhd->bnhgqk', qb, kn,
                   preferred_element_type=jnp.float32) * (HEAD_DIM ** -0.5)
    qi = jnp.arange(BLOCK)[:, None]
    kj = jnp.arange(3 * BLOCK)[None, :]
    rel = kj - BLOCK - qi
    kpos = jnp.arange(nb)[:, None] * BLOCK - BLOCK + jnp.arange(3 * BLOCK)[None, :]
    valid = (jnp.abs(rel) <= WINDOW)[None] & ((kpos >= 0) & (kpos < L))[:, None, :]
    s = jnp.where(valid[None, :, None, None], s, NEG)
    sink_b = sink.astype(jnp.float32).reshape(N_KV_HEADS, G)[None, None, :, :, None, None]
    m = jnp.maximum(jnp.max(s, axis=-1, keepdims=True), sink_b)
    p = jnp.exp(s - m)
    p = p / (jnp.sum(p, axis=-1, keepdims=True) + jnp.exp(sink_b - m))
    o = jnp.einsum('bnhgqk,bnkhd->bnqhgd', p.astype(v.dtype), vn)
    return o.reshape(B, L, ATTN_WIDTH)


def _ssm_combine(e1, e2):
    a1, b1 = e1
    a2, b2 = e2
    return a1 * a2, a2 * b1 + b2


def s5_bidirectional(u, a_re, a_im, log_dt, b_re, b_im, c_re, c_im, d_skip, w_glu, b_glu):
    B, L = u.shape[0], u.shape[1]
    ug = u.astype(jnp.float32).reshape(B, L, SSM_GROUPS, SSM_GROUP)
    ugc = ug.astype(jnp.complex64)
    lam = lax.complex(a_re.astype(jnp.float32), a_im.astype(jnp.float32))
    dt = jnp.exp(log_dt.astype(jnp.float32))[..., None]
    lam_bar = jnp.exp(lam * dt)
    bmat = lax.complex(b_re.astype(jnp.float32), b_im.astype(jnp.float32))
    b_bar = ((lam_bar - 1.0) / lam)[..., None] * bmat
    cmat = lax.complex(c_re.astype(jnp.float32), c_im.astype(jnp.float32))
    y = d_skip.astype(jnp.float32).reshape(SSM_GROUPS, SSM_GROUP) * ug
    for z in range(N_DIR):
        bu = jnp.einsum('gph,blgh->blgp', b_bar[z], ugc)
        a = jnp.broadcast_to(lam_bar[z], bu.shape)
        _, state = lax.associative_scan(_ssm_combine, (a, bu), axis=1, reverse=(z == 1))
        y = y + jnp.real(jnp.einsum('ghp,blgp->blgh', cmat[z], state))
    y = jax.nn.gelu(y.reshape(B, L, SSM_WIDTH), approximate=False)
    gate = jax.nn.sigmoid(jnp.einsum('blc,ce->ble', y, w_glu.astype(jnp.float32)) + b_glu.astype(jnp.float32))
    return (y * gate).astype(u.dtype)


def peer(h, w_query, sub_keys, expert_u, expert_v):
    B, L, D = h.shape
    T = B * L
    t = h.reshape(T, D)
    q = (t @ w_query).reshape(T, PEER_HEADS, 2, PEER_HALF)
    s = jnp.einsum('thzc,zkc->thzk', q, sub_keys, preferred_element_type=jnp.float32)
    s1, i1 = lax.top_k(s[:, :, 0], PEER_TOPK)
    s2, i2 = lax.top_k(s[:, :, 1], PEER_TOPK)
    cand = (s1[..., :, None] + s2[..., None, :]).reshape(T, PEER_HEADS, PEER_TOPK * PEER_TOPK)
    cand_idx = (i1[..., :, None] * PEER_KEYS + i2[..., None, :]).reshape(T, PEER_HEADS, PEER_TOPK * PEER_TOPK)
    top_s, sel = lax.top_k(cand, PEER_TOPK)
    idx = jnp.take_along_axis(cand_idx, sel, axis=-1)
    gate = jax.nn.softmax(top_s, axis=-1)
    n_chunks = T // PEER_CHUNK

    def chunk_fn(args):
        tc, ic, gc = args
        u = expert_u[ic]
        act = jax.nn.gelu(jnp.einsum('cd,chkd->chk', tc, u, preferred_element_type=jnp.float32),
                          approximate=False)
        w = (gc * act).astype(tc.dtype)
        return jnp.einsum('chk,chkd->cd', w, expert_v[ic])

    out = lax.map(chunk_fn, (t.reshape(n_chunks, PEER_CHUNK, D),
                             idx.reshape(n_chunks, PEER_CHUNK, PEER_HEADS, PEER_TOPK),
                             gate.reshape(n_chunks, PEER_CHUNK, PEER_HEADS, PEER_TOPK)))
    return out.reshape(B, L, D)


def encoder(x, norm_mix, w_in, attn_sink, w_attn_up, ssm_a_re, ssm_a_im, ssm_log_dt,
            ssm_b_re, ssm_b_im, ssm_c_re, ssm_c_im, ssm_d, w_glu, b_glu, w_ssm_up, w_out,
            norm_ffn, peer_w_query, peer_sub_keys, peer_u, peer_v, final_norm):
    B, L, _ = x.shape
    pos = jnp.arange(L)
    cuts = [ATTN_WIDTH, ATTN_WIDTH + KV_WIDTH, ATTN_WIDTH + 2 * KV_WIDTH,
            ATTN_WIDTH + 2 * KV_WIDTH + SSM_WIDTH, ATTN_WIDTH + 2 * KV_WIDTH + SSM_WIDTH + D_MODEL]
    for l in range(DEPTH):
        h = rms_norm(x, norm_mix[l])
        proj = jnp.einsum('bld,de->ble', h, w_in[l])
        q, k, v, s_in, g_a, g_s = jnp.split(proj, cuts, axis=-1)
        q = partial_rope(q.reshape(B, L, N_Q_HEADS, HEAD_DIM), pos)
        k = partial_rope(k.reshape(B, L, N_KV_HEADS, HEAD_DIM), pos)
        v = v.reshape(B, L, N_KV_HEADS, HEAD_DIM)
        attn = window_attention(q, k, v, attn_sink[l])
        ssm = s5_bidirectional(s_in, ssm_a_re[l], ssm_a_im[l], ssm_log_dt[l], ssm_b_re[l],
                               ssm_b_im[l], ssm_c_re[l], ssm_c_im[l], ssm_d[l], w_glu[l], b_glu[l])
        ya = attn @ w_attn_up[l]
        ys = ssm @ w_ssm_up[l]
        mixed = jax.nn.sigmoid(g_a) * ya + jax.nn.sigmoid(g_s) * ys
        x = x + mixed @ w_out[l]
        h = rms_norm(x, norm_ffn[l])
        x = x + peer(h, peer_w_query[l], peer_sub_keys[l], peer_u[l], peer_v[l])
    return rms_norm(x, final_norm)


def setup_inputs(seed: int = 0) -> dict:
    key = jax.random.key(seed)
    ks = jax.random.split(key, 26)
    f = jnp.float32
    nrm = lambda k, shape, std: jax.random.normal(k, shape, f) * std
    n_idx = jnp.arange(SSM_STATE, dtype=f)
    return {
        'x_prompt': nrm(ks[0], (BATCH, SEQ, D_MODEL), 1.0),
        'x_sample': nrm(ks[1], (DEC_BATCH, DEC_SEQ, D_MODEL), 1.0),
        'norm_mix': 1.0 + nrm(ks[2], (DEPTH, D_MODEL), 0.05),
        'w_in': nrm(ks[3], (DEPTH, D_MODEL, IN_WIDTH), D_MODEL ** -0.5),
        'attn_sink': nrm(ks[4], (DEPTH, N_Q_HEADS), 0.5),
        'w_attn_up': nrm(ks[5], (DEPTH, ATTN_WIDTH, D_MODEL), ATTN_WIDTH ** -0.5),
        'ssm_a_re': -0.5 + nrm(ks[6], (DEPTH, N_DIR, SSM_GROUPS, SSM_STATE), 0.01),
        'ssm_a_im': math.pi * n_idx + nrm(ks[7], (DEPTH, N_DIR, SSM_GROUPS, SSM_STATE), 0.01),
        'ssm_log_dt': jax.random.uniform(ks[8], (DEPTH, N_DIR, SSM_GROUPS), f,
                                         minval=math.log(1e-3), maxval=math.log(1e-1)),
        'ssm_b_re': nrm(ks[9], (DEPTH, N_DIR, SSM_GROUPS, SSM_STATE, SSM_GROUP), (2 * SSM_GROUP) ** -0.5),
        'ssm_b_im': nrm(ks[10], (DEPTH, N_DIR, SSM_GROUPS, SSM_STATE, SSM_GROUP), (2 * SSM_GROUP) ** -0.5),
        'ssm_c_re': nrm(ks[11], (DEPTH, N_DIR, SSM_GROUPS, SSM_GROUP, SSM_STATE), (2 * SSM_STATE) ** -0.5),
        'ssm_c_im': nrm(ks[12], (DEPTH, N_DIR, SSM_GROUPS, SSM_GROUP, SSM_STATE), (2 * SSM_STATE) ** -0.5),
        'ssm_d': nrm(ks[13], (DEPTH, SSM_WIDTH), 1.0),
        'w_glu': nrm(ks[14], (DEPTH, SSM_WIDTH, SSM_WIDTH), SSM_WIDTH ** -0.5),
        'b_glu': nrm(ks[15], (DEPTH, SSM_WIDTH), 0.02),
        'w_ssm_up': nrm(ks[16], (DEPTH, SSM_WIDTH, D_MODEL), SSM_WIDTH ** -0.5),
        'w_out': nrm(ks[17], (DEPTH, D_MODEL, D_MODEL), D_MODEL ** -0.5),
        'norm_ffn': 1.0 + nrm(ks[18], (DEPTH, D_MODEL), 0.05),
        'peer_w_query': nrm(ks[19], (DEPTH, D_MODEL, PEER_HEADS * PEER_QDIM), D_MODEL ** -0.5),
        'peer_sub_keys': nrm(ks[20], (DEPTH, 2, PEER_KEYS, PEER_HALF), PEER_HALF ** -0.5),
        'peer_u': nrm(ks[21], (DEPTH, PEER_EXPERTS, D_MODEL), D_MODEL ** -0.5),
        'peer_v': nrm(ks[22], (DEPTH, PEER_EXPERTS, D_MODEL), PEER_HEADS ** -0.5),
        'final_norm': 1.0 + nrm(ks[23], (D_MODEL,), 0.05),
    }


def reference(x_prompt, x_sample, norm_mix, w_in, attn_sink, w_attn_up, ssm_a_re, ssm_a_im,
              ssm_log_dt, ssm_b_re, ssm_b_im, ssm_c_re, ssm_c_im, ssm_d, w_glu, b_glu, w_ssm_up,
              w_out, norm_ffn, peer_w_query, peer_sub_keys, peer_u, peer_v, final_norm):
    y_prompt = encoder(x_prompt, norm_mix, w_in, attn_sink, w_attn_up, ssm_a_re, ssm_a_im, ssm_log_dt,
                       ssm_b_re, ssm_b_im, ssm_c_re, ssm_c_im, ssm_d, w_glu, b_glu, w_ssm_up, w_out,
                       norm_ffn, peer_w_query, peer_sub_keys, peer_u, peer_v, final_norm)
    y_sample = encoder(x_sample, norm_mix, w_in, attn_sink, w_attn_up, ssm_a_re, ssm_a_im, ssm_log_dt,
                       ssm_b_re, ssm_b_im, ssm_c_re, ssm_c_im, ssm_d, w_glu, b_glu, w_ssm_up, w_out,
                       norm_ffn, peer_w_query, peer_sub_keys, peer_u, peer_v, final_norm)
    return (y_prompt, y_sample)
```

```python
import functools
import math
from typing import NamedTuple, Tuple

import jax
import jax.numpy as jnp
from jax import lax
from jax.experimental import pallas as pl
from jax.experimental.pallas import tpu as pltpu

F32 = jnp.float32
BF16 = jnp.bfloat16

LANES = 128
SUBLANES = 8
MXU_DIM = 256
VMEM_LIMIT_BYTES = 56 * 1024 * 1024


class Cfg(NamedTuple):
    d_model: int = 4096
    seq_lens: Tuple[int, ...] = (8192, 4096, 4096)
    head_dim: int = 128
    n_q: int = 16
    n_kv: int = 4
    window: int = 128
    rope_dim: int = 32
    rope_theta: float = 500000.0
    ssm_width: int = 2048
    ssm_group: int = 16
    ssm_state: int = 64
    peer_heads: int = 8
    peer_keys: int = 128
    peer_topk: int = 16
    peer_qdim: int = 256
    eps: float = 1e-6
    neg: float = -1e30
    tm: int = 512
    tn: int = 1024
    scan_rows: int = 256
    topk_tokens: int = 256
    peer_tokens: int = 8

    @property
    def tokens(self):
        return sum(self.seq_lens)

    @property
    def attn_width(self):
        return self.n_q * self.head_dim

    @property
    def kv_width(self):
        return self.n_kv * self.head_dim


def _cparams(sem):
    return pltpu.CompilerParams(dimension_semantics=sem, vmem_limit_bytes=VMEM_LIMIT_BYTES)


def _sigmoid(x):
    return 1.0 / (1.0 + jnp.exp(-x))


def _gelu_exact(x):
    return 0.5 * x * (1.0 + lax.erf(x * (2.0 ** -0.5)))


def _seq_block_flags(blk, seq_lens, rows):
    first = None
    last = None
    start = 0
    for n in seq_lens:
        f = blk == (start // rows)
        l = blk == ((start + n) // rows - 1)
        first = f if first is None else jnp.logical_or(first, f)
        last = l if last is None else jnp.logical_or(last, l)
        start += n
    return first, last


def _norm_matmul_kernel(x_ref, g_ref, w_ref, *rest, eps, emit_h):
    if emit_h:
        o_ref, h_out_ref, h_scr = rest
    else:
        o_ref, h_scr = rest

    @pl.when(pl.program_id(1) == 0)
    def _():
        x = x_ref[...]
        ms = jnp.mean(x * x, axis=-1, keepdims=True)
        h = x * lax.rsqrt(ms + eps) * g_ref[...]
        h_scr[...] = h.astype(BF16)
        if emit_h:
            h_out_ref[...] = h

    o_ref[...] = jnp.dot(h_scr[...], w_ref[...], preferred_element_type=F32).astype(o_ref.dtype)


def norm_matmul(x, g, w, *, cfg, out_dtype, emit_h):
    T, D = x.shape
    N = w.shape[1]
    tm = min(cfg.tm // 2 if emit_h else cfg.tm, T)
    tn = min(cfg.tn, N)
    out_shape = [jax.ShapeDtypeStruct((T, N), out_dtype)]
    out_specs = [pl.BlockSpec((tm, tn), lambda i, j: (i, j))]
    if emit_h:
        out_shape.append(jax.ShapeDtypeStruct((T, D), F32))
        out_specs.append(pl.BlockSpec((tm, D), lambda i, j: (i, 0)))
    res = pl.pallas_call(
        functools.partial(_norm_matmul_kernel, eps=cfg.eps, emit_h=emit_h),
        grid=(T // tm, N // tn),
        in_specs=[
            pl.BlockSpec((tm, D), lambda i, j: (i, 0)),
            pl.BlockSpec((1, D), lambda i, j: (0, 0)),
            pl.BlockSpec((D, tn), lambda i, j: (0, j)),
        ],
        out_specs=out_specs,
        out_shape=out_shape,
        scratch_shapes=[pltpu.VMEM((tm, D), BF16)],
        compiler_params=_cparams(("arbitrary", "arbitrary")),
        name="norm_matmul_h" if emit_h else "norm_matmul",
    )(x, g.reshape(1, D).astype(F32), w)
    return res if emit_h else res[0]


def _rope(x, tab, hd, half):
    c = tab[:, 0:hd]
    sa = tab[:, hd:2 * hd]
    sb = tab[:, 2 * hd:3 * hd]
    return x * c + pltpu.roll(x, hd - half, axis=1) * sa + pltpu.roll(x, half, axis=1) * sb


def _attn_kernel(sink_ref, q_ref, kp_ref, kc_ref, kn_ref, vp_ref, vc_ref, vn_ref,
                 tp_ref, tc_ref, tn_ref, o_ref, *, cfg):
    hd = cfg.head_dim
    half = cfg.rope_dim // 2
    blk = cfg.window
    G = cfg.n_q // cfg.n_kv
    b = pl.program_id(0)
    first, last = _seq_block_flags(b, cfg.seq_lens, blk)
    lo = jnp.where(first, blk, 0)
    hi = jnp.where(last, 2 * blk, 3 * blk)
    kj = lax.broadcasted_iota(jnp.int32, (blk, 3 * blk), 1)
    qi = lax.broadcasted_iota(jnp.int32, (blk, 3 * blk), 0)
    rel = kj - blk - qi
    valid = (jnp.abs(rel) <= cfg.window) & (kj >= lo) & (kj < hi)
    scale = hd ** -0.5
    tabs = (tp_ref[...], tc_ref[...], tn_ref[...])
    tq = tabs[1]
    for j in range(cfg.n_kv):
        cs = slice(j * hd, (j + 1) * hd)
        ks = [_rope(r[:, cs].astype(F32), t, hd, half).astype(BF16)
              for r, t in zip((kp_ref, kc_ref, kn_ref), tabs)]
        kn = jnp.concatenate(ks, axis=0)
        vn = jnp.concatenate([vp_ref[:, cs], vc_ref[:, cs], vn_ref[:, cs]], axis=0)
        for g in range(G):
            h = j * G + g
            hs = slice(h * hd, (h + 1) * hd)
            qh = _rope(q_ref[:, hs].astype(F32), tq, hd, half).astype(BF16)
            s = lax.dot_general(qh, kn, (((1,), (1,)), ((), ())),
                                preferred_element_type=F32) * scale
            s = jnp.where(valid, s, cfg.neg)
            sink = sink_ref[h]
            m = jnp.maximum(jnp.max(s, axis=-1, keepdims=True), sink)
            p = jnp.exp(s - m)
            den = jnp.sum(p, axis=-1, keepdims=True) + jnp.exp(sink - m)
            p = p / den
            o = jnp.dot(p.astype(BF16), vn, preferred_element_type=F32)
            o_ref[:, hs] = o.astype(o_ref.dtype)


def window_attention(proj, sink, rope_tab, *, cfg, q_blk, k_blk, v_blk):
    T = proj.shape[0]
    blk = cfg.window
    nb = T // blk
    hd = cfg.head_dim
    prev = lambda b: jnp.maximum(b - 1, 0)
    nxt = lambda b: jnp.minimum(b + 1, nb - 1)
    kvw = cfg.kv_width
    kspec = lambda f, c: pl.BlockSpec((blk, kvw), lambda b: (f(b), c))
    tspec = lambda f: pl.BlockSpec((blk, 3 * hd), lambda b: (f(b), 0))
    ident = lambda b: b
    return pl.pallas_call(
        functools.partial(_attn_kernel, cfg=cfg),
        grid=(nb,),
        in_specs=[
            pl.BlockSpec(memory_space=pltpu.SMEM),
            pl.BlockSpec((blk, cfg.attn_width), lambda b: (b, q_blk)),
            kspec(prev, k_blk), kspec(ident, k_blk), kspec(nxt, k_blk),
            kspec(prev, v_blk), kspec(ident, v_blk), kspec(nxt, v_blk),
            tspec(prev), tspec(ident), tspec(nxt),
        ],
        out_specs=pl.BlockSpec((blk, cfg.attn_width), lambda b: (b, 0)),
        out_shape=jax.ShapeDtypeStruct((T, cfg.attn_width), BF16),
        compiler_params=_cparams(("arbitrary",)),
        name="window_attention",
    )(sink.astype(F32), proj, proj, proj, proj, proj, proj, proj, rope_tab, rope_tab, rope_tab)


def rope_table(cfg):
    hd = cfg.head_dim
    half = cfg.rope_dim // 2
    inv = cfg.rope_theta ** (-jnp.arange(half, dtype=F32) / half)
    tabs = []
    for n in cfg.seq_lens:
        ang = jnp.arange(n).astype(F32)[:, None] * inv[None, :]
        cos, sin = jnp.cos(ang), jnp.sin(ang)
        z = jnp.zeros((n, hd - 2 * half), F32)
        zh = jnp.zeros((n, half), F32)
        c = jnp.concatenate([cos, cos, jnp.ones((n, hd - 2 * half), F32)], axis=1)
        sa = jnp.concatenate([-sin, zh, z], axis=1)
        sb = jnp.concatenate([zh, sin, z], axis=1)
        tabs.append(jnp.concatenate([c, sa, sb], axis=1))
    return jnp.concatenate(tabs, axis=0)


SCAN_UNROLL = 8


def _s5_kernel(u_ref, b_ref, c_ref, a_ref, y_ref, bu_scr, st_scr, *, cfg, npack, nblk):
    rows = cfg.scan_rows
    pk_in = u_ref.shape[1] // npack
    ns = bu_scr.shape[1] // 2
    d = pl.program_id(0)
    i = pl.program_id(1)
    blk = jnp.where(d == 0, i, nblk - 1 - i)
    first, last = _seq_block_flags(blk, cfg.seq_lens, rows)
    reset = jnp.where(d == 0, first, last)

    @pl.when(reset)
    def _():
        st_scr[...] = jnp.zeros_like(st_scr)

    for j in range(npack):
        bu_scr[...] = jnp.dot(u_ref[:, j * pk_in:(j + 1) * pk_in], b_ref[0, j],
                              preferred_element_type=F32)
        ar = a_ref[0, j, 0:1, :]
        ai = a_ref[0, j, 1:2, :]

        def body(t8, carry):
            xr, xi = carry
            for k in range(SCAN_UNROLL):
                t = t8 * SCAN_UNROLL + k
                row = jnp.where(d == 0, t, rows - 1 - t)
                br = bu_scr[pl.ds(row, 1), 0:ns]
                bi = bu_scr[pl.ds(row, 1), ns:2 * ns]
                nxr = ar * xr - ai * xi + br
                nxi = ar * xi + ai * xr + bi
                bu_scr[pl.ds(row, 1), 0:ns] = nxr
                bu_scr[pl.ds(row, 1), ns:2 * ns] = nxi
                xr, xi = nxr, nxi
            return xr, xi

        xr, xi = lax.fori_loop(0, rows // SCAN_UNROLL, body,
                               (st_scr[j, 0:1, :], st_scr[j, 1:2, :]))
        st_scr[j, 0:1, :] = xr
        st_scr[j, 1:2, :] = xi
        y_ref[0, :, j * pk_in:(j + 1) * pk_in] = jnp.dot(
            bu_scr[...].astype(BF16), c_ref[0, j], preferred_element_type=F32)


def s5_scan(proj, bpack, cpack, apack, *, cfg, u_blk):
    T = proj.shape[0]
    W = cfg.ssm_width
    rows = cfg.scan_rows
    nblk = T // rows
    npack = bpack.shape[1]
    ncol = bpack.shape[3]
    order = lambda d, i: jnp.where(d == 0, i, nblk - 1 - i)
    return pl.pallas_call(
        functools.partial(_s5_kernel, cfg=cfg, npack=npack, nblk=nblk),
        grid=(2, nblk),
        in_specs=[
            pl.BlockSpec((rows, W), lambda d, i: (order(d, i), u_blk)),
            pl.BlockSpec((1,) + bpack.shape[1:], lambda d, i: (d, 0, 0, 0)),
            pl.BlockSpec((1,) + cpack.shape[1:], lambda d, i: (d, 0, 0, 0)),
            pl.BlockSpec((1,) + apack.shape[1:], lambda d, i: (d, 0, 0, 0)),
        ],
        out_specs=pl.BlockSpec((1, rows, W), lambda d, i: (d, order(d, i), 0)),
        out_shape=jax.ShapeDtypeStruct((2, T, W), F32),
        scratch_shapes=[pltpu.VMEM((rows, ncol), F32),
                        pltpu.VMEM((npack, 2, ncol // 2), F32)],
        compiler_params=_cparams(("arbitrary", "arbitrary")),
        name="s5_scan",
    )(proj, bpack, cpack, apack)


def s5_params(a_re, a_im, log_dt, b_re, b_im, c_re, c_im, *, cfg):
    H = cfg.ssm_group
    P = cfg.ssm_state
    Gn = cfg.ssm_width // H
    gpp = MXU_DIM // H
    npack = Gn // gpp
    lam = lax.complex(a_re.astype(F32), a_im.astype(F32))
    dt = jnp.exp(log_dt.astype(F32))[..., None]
    lam_bar = jnp.exp(lam * dt)
    bmat = lax.complex(b_re.astype(F32), b_im.astype(F32))
    b_bar = ((lam_bar - 1.0) / lam)[..., None] * bmat
    eye = jnp.eye(gpp, dtype=F32)

    def blockdiag(m):
        z, n, g, r, c = m.shape
        return jnp.einsum('zngrc,gk->zngrkc', m, eye).reshape(z, n, g * r, g * c)

    bt = jnp.swapaxes(b_bar, -1, -2).reshape(2, npack, gpp, H, P)
    bpack = jnp.concatenate([blockdiag(jnp.real(bt)), blockdiag(jnp.imag(bt))], axis=-1)
    ct = jnp.swapaxes(lax.complex(c_re.astype(F32), c_im.astype(F32)), -1, -2)
    ct = ct.reshape(2, npack, gpp, P, H)
    cpack = jnp.concatenate([blockdiag(jnp.real(ct)), blockdiag(-jnp.imag(ct))], axis=-2)
    lb = lam_bar.reshape(2, npack, 1, gpp * P)
    apack = jnp.concatenate([jnp.real(lb), jnp.imag(lb)], axis=2)
    return bpack.astype(BF16), cpack.astype(BF16), apack.astype(F32)


def _ssm_post_kernel(u_ref, y_ref, d_ref, wg_ref, bg_ref, o_ref):
    y = d_ref[...] * u_ref[...].astype(F32) + y_ref[0] + y_ref[1]
    y = _gelu_exact(y)
    z = jnp.dot(y.astype(BF16), wg_ref[...], preferred_element_type=F32) + bg_ref[...]
    o_ref[...] = (y * _sigmoid(z)).astype(o_ref.dtype)


def ssm_post(proj, y2, d_skip, w_glu, b_glu, *, cfg, u_blk):
    T = proj.shape[0]
    W = cfg.ssm_width
    tm = min(cfg.tm, T)
    return pl.pallas_call(
        _ssm_post_kernel,
        grid=(T // tm,),
        in_specs=[
            pl.BlockSpec((tm, W), lambda i: (i, u_blk)),
            pl.BlockSpec((2, tm, W), lambda i: (0, i, 0)),
            pl.BlockSpec((1, W), lambda i: (0, 0)),
            pl.BlockSpec((W, W), lambda i: (0, 0)),
            pl.BlockSpec((1, W), lambda i: (0, 0)),
        ],
        out_specs=pl.BlockSpec((tm, W), lambda i: (i, 0)),
        out_shape=jax.ShapeDtypeStruct((T, W), BF16),
        compiler_params=_cparams(("arbitrary",)),
        name="ssm_post",
    )(proj, y2, d_skip.reshape(1, W).astype(F32), w_glu, b_glu.reshape(1, W).astype(F32))


def _mix_kernel(a_ref, s_ref, wa_ref, ws_ref, ga_ref, gs_ref, o_ref):
    ya = jnp.dot(a_ref[...], wa_ref[...], preferred_element_type=F32)
    ys = jnp.dot(s_ref[...], ws_ref[...], preferred_element_type=F32)
    o_ref[...] = (_sigmoid(ga_ref[...].astype(F32)) * ya
                  + _sigmoid(gs_ref[...].astype(F32)) * ys).astype(o_ref.dtype)


def gated_mix(attn, ssm, wa, ws, proj, *, cfg, ga_off, gs_off):
    T, D = attn.shape[0], cfg.d_model
    tm = min(cfg.tm, T)
    tn = min(cfg.tn, D)
    return pl.pallas_call(
        _mix_kernel,
        grid=(T // tm, D // tn),
        in_specs=[
            pl.BlockSpec((tm, attn.shape[1]), lambda i, j: (i, 0)),
            pl.BlockSpec((tm, ssm.shape[1]), lambda i, j: (i, 0)),
            pl.BlockSpec((wa.shape[0], tn), lambda i, j: (0, j)),
            pl.BlockSpec((ws.shape[0], tn), lambda i, j: (0, j)),
            pl.BlockSpec((tm, tn), lambda i, j: (i, ga_off // tn + j)),
            pl.BlockSpec((tm, tn), lambda i, j: (i, gs_off // tn + j)),
        ],
        out_specs=pl.BlockSpec((tm, tn), lambda i, j: (i, j)),
        out_shape=jax.ShapeDtypeStruct((T, D), BF16),
        compiler_params=_cparams(("arbitrary", "arbitrary")),
        name="gated_mix",
    )(attn, ssm, wa, ws, proj, proj)


def _out_proj_kernel(x_ref, m_ref, w_ref, o_ref):
    o_ref[...] = x_ref[...] + jnp.dot(m_ref[...], w_ref[...], preferred_element_type=F32)


def out_proj(x, mixed, w, *, cfg):
    T, D = x.shape
    tm = min(cfg.tm, T)
    tn = min(cfg.tn, D)
    return pl.pallas_call(
        _out_proj_kernel,
        grid=(T // tm, D // tn),
        in_specs=[
            pl.BlockSpec((tm, tn), lambda i, j: (i, j)),
            pl.BlockSpec((tm, D), lambda i, j: (i, 0)),
            pl.BlockSpec((D, tn), lambda i, j: (0, j)),
        ],
        out_specs=pl.BlockSpec((tm, tn), lambda i, j: (i, j)),
        out_shape=jax.ShapeDtypeStruct((T, D), F32),
        compiler_params=_cparams(("arbitrary", "arbitrary")),
        name="out_proj",
    )(x, mixed, w)


def _topk_cols(s, k, payload=None):
    n = s.shape[0]
    row = lax.broadcasted_iota(jnp.int32, s.shape, 0).astype(F32)
    vals, picks = [], []
    for _ in range(k):
        m = jnp.max(s, axis=0, keepdims=True)
        am = jnp.min(jnp.where(s == m, row, float(n)), axis=0, keepdims=True)
        hit = row == am
        vals.append(m)
        if payload is None:
            picks.append(am)
        else:
            picks.append(jnp.sum(jnp.where(hit, payload, 0.0), axis=0, keepdims=True))
        s = jnp.where(hit, -jnp.inf, s)
    return jnp.concatenate(vals, axis=0), jnp.concatenate(picks, axis=0)


def _peer_topk_kernel(q_ref, keys_ref, idx_ref, gate_ref, *, cfg):
    K = cfg.peer_topk
    nk = cfg.peer_keys
    half = cfg.peer_qdim // 2
    for hd in range(cfg.peer_heads):
        tops = []
        for z in range(2):
            c0 = (hd * 2 + z) * half
            q = q_ref[:, c0:c0 + half]
            s = lax.dot_general(keys_ref[z], q, (((1,), (1,)), ((), ())),
                                preferred_element_type=F32,
                                precision=lax.Precision.HIGHEST)
            tops.append(_topk_cols(s, K))
        (s1, i1), (s2, i2) = tops
        cand = jnp.concatenate([s1[a:a + 1, :] + s2 for a in range(K)], axis=0)
        cidx = jnp.concatenate([i1[a:a + 1, :] * float(nk) + i2 for a in range(K)], axis=0)
        top_s, sel = _topk_cols(cand, K, payload=cidx)
        e = jnp.exp(top_s - top_s[0:1, :])
        gate = e / jnp.sum(e, axis=0, keepdims=True)
        idx_ref[hd * K:(hd + 1) * K, :] = sel.astype(jnp.int32)
        gate_ref[hd * K:(hd + 1) * K, :] = gate


def peer_topk(q, sub_keys, *, cfg):
    T = q.shape[0]
    tt = min(cfg.topk_tokens, T)
    R = cfg.peer_heads * cfg.peer_topk
    return pl.pallas_call(
        functools.partial(_peer_topk_kernel, cfg=cfg),
        grid=(T // tt,),
        in_specs=[
            pl.BlockSpec((tt, q.shape[1]), lambda i: (i, 0)),
            pl.BlockSpec(sub_keys.shape, lambda i: (0, 0, 0)),
        ],
        out_specs=[pl.BlockSpec((R, tt), lambda i: (0, i)),
                   pl.BlockSpec((R, tt), lambda i: (0, i))],
        out_shape=[jax.ShapeDtypeStruct((R, T), jnp.int32),
                   jax.ShapeDtypeStruct((R, T), F32)],
        compiler_params=_cparams(("arbitrary",)),
        name="peer_topk",
    )(q, sub_keys.astype(F32))


def pack_expert_tables(u, v):
    def pack(t):
        n, d = t.shape
        bits = lax.bitcast_convert_type(t.astype(BF16), jnp.uint16).astype(jnp.uint32)
        return bits[:, :d // 2] | (bits[:, d // 2:] << 16)
    return jnp.concatenate([pack(u), pack(v)], axis=1)


def _peer_eval_kernel(idx_cur_ref, idx_nxt_ref, gate_ref, h_ref, x_ref, g_ref, tab_ref, o_ref,
                      buf, sem, r_scr, out_scr, *, cfg, nsteps):
    nt = cfg.peer_tokens
    R = cfg.peer_heads * cfg.peer_topk
    D = cfg.d_model
    Dh = D // 2
    i = pl.program_id(0)
    slot = i % 2

    def issue(idx_ref, tok, dst_slot):
        for r in range(R):
            e = idx_ref[0, 0, tok * R + r]
            pltpu.make_async_copy(tab_ref.at[pl.ds(e, 1), :],
                                  buf.at[dst_slot, pl.ds(tok * R + r, 1), :],
                                  sem.at[dst_slot]).start()

    @pl.when(i == 0)
    def _():
        for tok in range(nt):
            issue(idx_cur_ref, tok, 0)

    pltpu.make_async_copy(tab_ref.at[pl.ds(0, nt * R), :], buf.at[slot], sem.at[slot]).wait()

    gate_t = gate_ref[...].T
    lane = lax.broadcasted_iota(jnp.int32, (R, LANES), 1)
    himask = jnp.uint32(0xFFFF0000)
    hl = LANES // 2
    for tok in range(nt):
        issue(idx_nxt_ref, tok, 1 - slot)

        base = tok * R
        view = pltpu.bitcast(buf[slot, base:base + R, 0:Dh], BF16)
        h = h_ref[tok:tok + 1, :].astype(BF16)
        hcat = jnp.concatenate([jnp.broadcast_to(h[:, 0:Dh], (hl, Dh)),
                                jnp.broadcast_to(h[:, Dh:D], (hl, Dh))], axis=0)
        r_scr[...] = lax.dot_general(view, hcat, (((1,), (1,)), ((), ())),
                                     preferred_element_type=F32)
        ra = r_scr[pl.ds(0, R, stride=2), :]
        rb = r_scr[pl.ds(1, R, stride=2), :]
        ah = ra + pltpu.roll(rb, hl, axis=1)
        act = jnp.where(lane < hl, ah, pltpu.roll(ah, hl, axis=1))
        coef = jnp.broadcast_to(gate_t[:, tok:tok + 1], (R, LANES)) * _gelu_exact(act)
        for c in range(Dh // LANES):
            acc_lo = jnp.zeros((SUBLANES, LANES), F32)
            acc_hi = jnp.zeros((SUBLANES, LANES), F32)
            for pr in range(R // SUBLANES):
                w = buf[slot, base + pr * SUBLANES:base + (pr + 1) * SUBLANES,
                        Dh + c * LANES:Dh + (c + 1) * LANES]
                cf = coef[pr * SUBLANES:(pr + 1) * SUBLANES, :]
                acc_lo = acc_lo + pltpu.bitcast(w << 16, F32) * cf
                acc_hi = acc_hi + pltpu.bitcast(w & himask, F32) * cf
            out_scr[tok:tok + 1, c * LANES:(c + 1) * LANES] = jnp.sum(acc_lo, axis=0, keepdims=True)
            out_scr[tok:tok + 1, Dh + c * LANES:Dh + (c + 1) * LANES] = jnp.sum(acc_hi, axis=0, keepdims=True)

    x2 = x_ref[...] + out_scr[...]
    ms = jnp.mean(x2 * x2, axis=-1, keepdims=True)
    o_ref[...] = x2 * lax.rsqrt(ms + cfg.eps) * g_ref[...]

    @pl.when(i == nsteps - 1)
    def _():
        pltpu.make_async_copy(tab_ref.at[pl.ds(0, nt * R), :], buf.at[1 - slot], sem.at[1 - slot]).wait()


def peer_eval(idx, gate, h2, x1, final_norm, table, *, cfg):
    T, D = x1.shape
    nt = cfg.peer_tokens
    R = cfg.peer_heads * cfg.peer_topk
    nsteps = T // nt
    idx3 = idx.reshape(nsteps, 1, nt * R)
    return pl.pallas_call(
        functools.partial(_peer_eval_kernel, cfg=cfg, nsteps=nsteps),
        grid=(nsteps,),
        in_specs=[
            pl.BlockSpec((1, 1, nt * R), lambda i: (i, 0, 0), memory_space=pltpu.SMEM),
            pl.BlockSpec((1, 1, nt * R), lambda i: (jnp.minimum(i + 1, nsteps - 1), 0, 0),
                         memory_space=pltpu.SMEM),
            pl.BlockSpec((nt, R), lambda i: (i, 0)),
            pl.BlockSpec((nt, D), lambda i: (i, 0)),
            pl.BlockSpec((nt, D), lambda i: (i, 0)),
            pl.BlockSpec((1, D), lambda i: (0, 0)),
            pl.BlockSpec(memory_space=pl.ANY),
        ],
        out_specs=pl.BlockSpec((nt, D), lambda i: (i, 0)),
        out_shape=jax.ShapeDtypeStruct((T, D), F32),
        scratch_shapes=[
            pltpu.VMEM((2, nt * R, D), jnp.uint32),
            pltpu.SemaphoreType.DMA((2,)),
            pltpu.VMEM((2 * R, LANES), F32),
            pltpu.VMEM((nt, D), F32),
        ],
        compiler_params=_cparams(("arbitrary",)),
        name="peer_eval",
    )(idx3, idx3, gate, h2, x1, final_norm.reshape(1, D).astype(F32), table)


def encoder_layer(x, p, *, cfg):
    aw, kw, sw, D = cfg.attn_width, cfg.kv_width, cfg.ssm_width, cfg.d_model
    w_in = p['w_in']
    c_q, c_k, c_v, c_s, c_ga = aw, aw + kw, aw + 2 * kw, aw + 2 * kw + sw, aw + 2 * kw + sw + D
    w_perm = jnp.concatenate([w_in[:, :c_q], w_in[:, c_v:c_s], w_in[:, c_s:c_ga], w_in[:, c_ga:],
                              w_in[:, c_q:c_k], w_in[:, c_k:c_v]], axis=1).astype(BF16)
    o_s, o_ga, o_gs, o_k, o_v = aw, aw + sw, aw + sw + D, aw + sw + 2 * D, aw + sw + 2 * D + kw
    assert o_s % sw == 0 and o_k % kw == 0 and o_v % kw == 0
    assert o_ga % min(cfg.tn, D) == 0 and o_gs % min(cfg.tn, D) == 0

    proj = norm_matmul(x, p['norm_mix'], w_perm, cfg=cfg, out_dtype=BF16, emit_h=False)
    attn = window_attention(proj, p['attn_sink'], rope_table(cfg), cfg=cfg,
                            q_blk=0, k_blk=o_k // kw, v_blk=o_v // kw)
    bpack, cpack, apack = s5_params(p['ssm_a_re'], p['ssm_a_im'], p['ssm_log_dt'], p['ssm_b_re'],
                                    p['ssm_b_im'], p['ssm_c_re'], p['ssm_c_im'], cfg=cfg)
    y2 = s5_scan(proj, bpack, cpack, apack, cfg=cfg, u_blk=o_s // sw)
    ssm = ssm_post(proj, y2, p['ssm_d'], p['w_glu'].astype(BF16), p['b_glu'], cfg=cfg, u_blk=o_s // sw)
    mixed = gated_mix(attn, ssm, p['w_attn_up'].astype(BF16), p['w_ssm_up'].astype(BF16), proj,
                      cfg=cfg, ga_off=o_ga, gs_off=o_gs)
    x1 = out_proj(x, mixed, p['w_out'].astype(BF16), cfg=cfg)
    q, h2 = norm_matmul(x1, p['norm_ffn'], p['peer_w_query'].astype(BF16), cfg=cfg,
                        out_dtype=F32, emit_h=True)
    idx_t, gate_t = peer_topk(q, p['peer_sub_keys'], cfg=cfg)
    table = pack_expert_tables(p['peer_u'], p['peer_v'])
    return peer_eval(idx_t.T, gate_t.T, h2, x1, p['final_norm'], table, cfg=cfg)


_PARAM_NAMES = ('norm_mix', 'w_in', 'attn_sink', 'w_attn_up', 'ssm_a_re', 'ssm_a_im', 'ssm_log_dt',
                'ssm_b_re', 'ssm_b_im', 'ssm_c_re', 'ssm_c_im', 'ssm_d', 'w_glu', 'b_glu', 'w_ssm_up',
                'w_out', 'norm_ffn', 'peer_w_query', 'peer_sub_keys', 'peer_u', 'peer_v')


def run_layer(x_prompt, x_sample, params, final_norm, cfg):
    D = cfg.d_model
    x = jnp.concatenate([x_prompt.reshape(-1, D), x_sample.reshape(-1, D)], axis=0)
    p = {k: v[0] for k, v in zip(_PARAM_NAMES, params)}
    p['final_norm'] = final_norm
    y = encoder_layer(x, p, cfg=cfg)
    n0 = x_prompt.shape[0] * x_prompt.shape[1]
    return y[:n0].reshape(x_prompt.shape), y[n0:].reshape(x_sample.shape)


def kernel(x_prompt, x_sample, norm_mix, w_in, attn_sink, w_attn_up, ssm_a_re, ssm_a_im, ssm_log_dt, ssm_b_re, ssm_b_im, ssm_c_re, ssm_c_im, ssm_d, w_glu, b_glu, w_ssm_up, w_out, norm_ffn, peer_w_query, peer_sub_keys, peer_u, peer_v, final_norm):
    b, s, d = x_prompt.shape
    db, ds, _ = x_sample.shape
    cfg = Cfg(d_model=d, seq_lens=(s,) * b + (ds,) * db)
    params = (norm_mix, w_in, attn_sink, w_attn_up, ssm_a_re, ssm_a_im, ssm_log_dt, ssm_b_re, ssm_b_im,
              ssm_c_re, ssm_c_im, ssm_d, w_glu, b_glu, w_ssm_up, w_out, norm_ffn, peer_w_query,
              peer_sub_keys, peer_u, peer_v)
    return run_layer(x_prompt, x_sample, params, final_norm, cfg)
```

```python
import functools
import math
from typing import NamedTuple, Tuple

import jax
import jax.numpy as jnp
from jax import lax
from jax.experimental import pallas as pl
from jax.experimental.pallas import tpu as pltpu

F32 = jnp.float32
BF16 = jnp.bfloat16

LANES = 128
SUBLANES = 8
MXU_DIM = 256
VMEM_LIMIT_BYTES = 56 * 1024 * 1024


class Cfg(NamedTuple):
    d_model: int = 4096
    seq_lens: Tuple[int, ...] = (8192, 4096, 4096)
    head_dim: int = 128
    n_q: int = 16
    n_kv: int = 4
    window: int = 128
    rope_dim: int = 32
    rope_theta: float = 500000.0
    ssm_width: int = 2048
    ssm_group: int = 16
    ssm_state: int = 64
    peer_heads: int = 8
    peer_keys: int = 128
    peer_topk: int = 16
    peer_qdim: int = 256
    eps: float = 1e-6
    neg: float = -1e30
    tm: int = 512
    tn: int = 1024
    scan_rows: int = 256
    topk_tokens: int = 256
    peer_tokens: int = 8

    @property
    def tokens(self):
        return sum(self.seq_lens)

    @property
    def attn_width(self):
        return self.n_q * self.head_dim

    @property
    def kv_width(self):
        return self.n_kv * self.head_dim


def _cparams(sem):
    return pltpu.CompilerParams(dimension_semantics=sem, vmem_limit_bytes=VMEM_LIMIT_BYTES)


def _sigmoid(x):
    return 1.0 / (1.0 + jnp.exp(-x))


def _gelu_exact(x):
    return 0.5 * x * (1.0 + lax.erf(x * (2.0 ** -0.5)))


def _seq_block_flags(blk, seq_lens, rows):
    first = None
    last = None
    start = 0
    for n in seq_lens:
        f = blk == (start // rows)
        l = blk == ((start + n) // rows - 1)
        first = f if first is None else jnp.logical_or(first, f)
        last = l if last is None else jnp.logical_or(last, l)
        start += n
    return first, last


def _row_block_ranges(xs, tm):
    out, start = [], 0
    for x in xs:
        assert x.shape[0] % tm == 0
        out.append((start, x.shape[0] // tm))
        start += x.shape[0] // tm
    return out, start


def _norm_matmul_kernel(*refs, eps, emit_h, ranges, tm):
    nx = len(ranges)
    x_hbm = refs[:nx]
    g_ref, w_ref = refs[nx:nx + 2]
    if emit_h:
        o_ref, h_out_ref, x_scr, h_scr, sem = refs[nx + 2:]
    else:
        o_ref, x_scr, h_scr, sem = refs[nx + 2:]
    i = pl.program_id(0)

    @pl.when(pl.program_id(1) == 0)
    def _():
        for k, (start, nblk) in enumerate(ranges):
            @pl.when(jnp.logical_and(i >= start, i < start + nblk))
            def _():
                cp = pltpu.make_async_copy(x_hbm[k].at[pl.ds((i - start) * tm, tm), :], x_scr, sem.at[0])
                cp.start()
                cp.wait()
        x = x_scr[...]
        ms = jnp.mean(x * x, axis=-1, keepdims=True)
        h = x * lax.rsqrt(ms + eps) * g_ref[...]
        h_scr[...] = h.astype(BF16)
        if emit_h:
            h_out_ref[...] = h

    o_ref[...] = jnp.dot(h_scr[...], w_ref[...], preferred_element_type=F32).astype(o_ref.dtype)


def norm_matmul(xs, g, w, *, cfg, out_dtype, emit_h):
    D = xs[0].shape[1]
    T = sum(x.shape[0] for x in xs)
    N = w.shape[1]
    tm = min(cfg.tm // 2 if emit_h else cfg.tm, min(x.shape[0] for x in xs))
    tn = min(cfg.tn, N)
    assert N % tn == 0
    ranges, nrow = _row_block_ranges(xs, tm)
    out_shape = [jax.ShapeDtypeStruct((T, N), out_dtype)]
    out_specs = [pl.BlockSpec((tm, tn), lambda i, j: (i, j))]
    if emit_h:
        out_shape.append(jax.ShapeDtypeStruct((T, D), F32))
        out_specs.append(pl.BlockSpec((tm, D), lambda i, j: (i, 0)))
    res = pl.pallas_call(
        functools.partial(_norm_matmul_kernel, eps=cfg.eps, emit_h=emit_h, ranges=ranges, tm=tm),
        grid=(nrow, N // tn),
        in_specs=[pl.BlockSpec(memory_space=pl.ANY)] * len(xs) + [
            pl.BlockSpec((1, D), lambda i, j: (0, 0)),
            pl.BlockSpec((D, tn), lambda i, j: (0, j)),
        ],
        out_specs=out_specs,
        out_shape=out_shape,
        scratch_shapes=[pltpu.VMEM((tm, D), F32), pltpu.VMEM((tm, D), BF16), pltpu.SemaphoreType.DMA((1,))],
        compiler_params=_cparams(("arbitrary", "arbitrary")),
        name="norm_matmul_h" if emit_h else "norm_matmul",
    )(*xs, g.reshape(1, D).astype(F32), w)
    return res if emit_h else res[0]


def _rope(x, tab, hd, half):
    c = tab[:, 0:hd]
    sa = tab[:, hd:2 * hd]
    sb = tab[:, 2 * hd:3 * hd]
    return x * c + pltpu.roll(x, hd - half, axis=1) * sa + pltpu.roll(x, half, axis=1) * sb


def _attn_kernel(sink_ref, q_ref, kp_ref, kc_ref, kn_ref, vp_ref, vc_ref, vn_ref,
                 tp_ref, tc_ref, tn_ref, o_ref, *, cfg):
    hd = cfg.head_dim
    half = cfg.rope_dim // 2
    blk = cfg.window
    G = cfg.n_q // cfg.n_kv
    b = pl.program_id(0)
    first, last = _seq_block_flags(b, cfg.seq_lens, blk)
    lo = jnp.where(first, blk, 0)
    hi = jnp.where(last, 2 * blk, 3 * blk)
    kj = lax.broadcasted_iota(jnp.int32, (blk, 3 * blk), 1)
    qi = lax.broadcasted_iota(jnp.int32, (blk, 3 * blk), 0)
    rel = kj - blk - qi
    valid = (jnp.abs(rel) <= cfg.window) & (kj >= lo) & (kj < hi)
    scale = hd ** -0.5
    tabs = (tp_ref[...], tc_ref[...], tn_ref[...])
    tq = tabs[1]
    for j in range(cfg.n_kv):
        cs = slice(j * hd, (j + 1) * hd)
        ks = [_rope(r[:, cs].astype(F32), t, hd, half).astype(BF16)
              for r, t in zip((kp_ref, kc_ref, kn_ref), tabs)]
        kn = jnp.concatenate(ks, axis=0)
        vn = jnp.concatenate([vp_ref[:, cs], vc_ref[:, cs], vn_ref[:, cs]], axis=0)
        for g in range(G):
            h = j * G + g
            hs = slice(h * hd, (h + 1) * hd)
            qh = _rope(q_ref[:, hs].astype(F32), tq, hd, half).astype(BF16)
            s = lax.dot_general(qh, kn, (((1,), (1,)), ((), ())),
                                preferred_element_type=F32) * scale
            s = jnp.where(valid, s, cfg.neg)
            sink = sink_ref[h]
            m = jnp.maximum(jnp.max(s, axis=-1, keepdims=True), sink)
            p = jnp.exp(s - m)
            den = jnp.sum(p, axis=-1, keepdims=True) + jnp.exp(sink - m)
            p = p / den
            o = jnp.dot(p.astype(BF16), vn, preferred_element_type=F32)
            o_ref[:, hs] = o.astype(o_ref.dtype)


def window_attention(proj, sink, rope_tab, *, cfg, q_blk, k_blk, v_blk):
    T = proj.shape[0]
    blk = cfg.window
    nb = T // blk
    hd = cfg.head_dim
    prev = lambda b: jnp.maximum(b - 1, 0)
    nxt = lambda b: jnp.minimum(b + 1, nb - 1)
    kvw = cfg.kv_width
    kspec = lambda f, c: pl.BlockSpec((blk, kvw), lambda b: (f(b), c))
    tspec = lambda f: pl.BlockSpec((blk, 3 * hd), lambda b: (f(b), 0))
    ident = lambda b: b
    return pl.pallas_call(
        functools.partial(_attn_kernel, cfg=cfg),
        grid=(nb,),
        in_specs=[
            pl.BlockSpec(memory_space=pltpu.SMEM),
            pl.BlockSpec((blk, cfg.attn_width), lambda b: (b, q_blk)),
            kspec(prev, k_blk), kspec(ident, k_blk), kspec(nxt, k_blk),
            kspec(prev, v_blk), kspec(ident, v_blk), kspec(nxt, v_blk),
            tspec(prev), tspec(ident), tspec(nxt),
        ],
        out_specs=pl.BlockSpec((blk, cfg.attn_width), lambda b: (b, 0)),
        out_shape=jax.ShapeDtypeStruct((T, cfg.attn_width), BF16),
        compiler_params=_cparams(("arbitrary",)),
        name="window_attention",
    )(sink.astype(F32), proj, proj, proj, proj, proj, proj, proj, rope_tab, rope_tab, rope_tab)


def rope_table(cfg):
    hd = cfg.head_dim
    half = cfg.rope_dim // 2
    inv = cfg.rope_theta ** (-jnp.arange(half, dtype=F32) / half)
    tabs = []
    for n in cfg.seq_lens:
        ang = jnp.arange(n).astype(F32)[:, None] * inv[None, :]
        cos, sin = jnp.cos(ang), jnp.sin(ang)
        z = jnp.zeros((n, hd - 2 * half), F32)
        zh = jnp.zeros((n, half), F32)
        c = jnp.concatenate([cos, cos, jnp.ones((n, hd - 2 * half), F32)], axis=1)
        sa = jnp.concatenate([-sin, zh, z], axis=1)
        sb = jnp.concatenate([zh, sin, z], axis=1)
        tabs.append(jnp.concatenate([c, sa, sb], axis=1))
    return jnp.concatenate(tabs, axis=0)


SCAN_UNROLL = 4


def _s5_kernel(u_ref, b_ref, c_ref, a_ref, y_ref, bu_scr, st_scr, *, cfg, npack, nblk):
    rows = cfg.scan_rows
    pk_in = u_ref.shape[1] // npack
    nch = bu_scr.shape[0] // 2
    ns = nch * LANES
    d = pl.program_id(0)
    i = pl.program_id(1)
    blk = jnp.where(d == 0, i, nblk - 1 - i)
    first, last = _seq_block_flags(blk, cfg.seq_lens, rows)
    reset = jnp.where(d == 0, first, last)

    @pl.when(reset)
    def _():
        st_scr[...] = jnp.zeros_like(st_scr)

    for j in range(npack):
        bu = jnp.dot(u_ref[:, j * pk_in:(j + 1) * pk_in], b_ref[0, j], preferred_element_type=F32)
        for c in range(2 * nch):
            bu_scr[c, pl.ds(j, rows, stride=npack), :] = bu[:, c * LANES:(c + 1) * LANES]

    ar = a_ref[0, 0]
    ai = a_ref[0, 1]

    def body(tu, carry):
        xr, xi = carry
        for k in range(SCAN_UNROLL):
            t = tu * SCAN_UNROLL + k
            row = jnp.where(d == 0, t, rows - 1 - t)
            base = pl.multiple_of(row * npack, npack)
            br = jnp.concatenate([bu_scr[c, pl.ds(base, npack), :] for c in range(nch)], axis=1)
            bi = jnp.concatenate([bu_scr[nch + c, pl.ds(base, npack), :] for c in range(nch)], axis=1)
            nxr = ar * xr - ai * xi + br
            nxi = ar * xi + ai * xr + bi
            for c in range(nch):
                bu_scr[c, pl.ds(base, npack), :] = nxr[:, c * LANES:(c + 1) * LANES]
                bu_scr[nch + c, pl.ds(base, npack), :] = nxi[:, c * LANES:(c + 1) * LANES]
            xr, xi = nxr, nxi
        return xr, xi

    xr, xi = lax.fori_loop(0, rows // SCAN_UNROLL, body, (st_scr[0], st_scr[1]))
    st_scr[0] = xr
    st_scr[1] = xi
    for j in range(npack):
        xs = jnp.concatenate([bu_scr[c, pl.ds(j, rows, stride=npack), :] for c in range(2 * nch)], axis=1)
        y_ref[0, :, j * pk_in:(j + 1) * pk_in] = jnp.dot(
            xs.astype(BF16), c_ref[0, j], preferred_element_type=F32)


def s5_scan(proj, bpack, cpack, apack, *, cfg, u_blk):
    T = proj.shape[0]
    W = cfg.ssm_width
    rows = cfg.scan_rows
    nblk = T // rows
    npack = bpack.shape[1]
    ncol = bpack.shape[3]
    assert npack == SUBLANES, "one pack of groups per sublane"
    order = lambda d, i: jnp.where(d == 0, i, nblk - 1 - i)
    once = dict(pipeline_mode=pl.Buffered(1))
    return pl.pallas_call(
        functools.partial(_s5_kernel, cfg=cfg, npack=npack, nblk=nblk),
        grid=(2, nblk),
        in_specs=[
            pl.BlockSpec((rows, W), lambda d, i: (order(d, i), u_blk)),
            pl.BlockSpec((1,) + bpack.shape[1:], lambda d, i: (d, 0, 0, 0), **once),
            pl.BlockSpec((1,) + cpack.shape[1:], lambda d, i: (d, 0, 0, 0), **once),
            pl.BlockSpec((1,) + apack.shape[1:], lambda d, i: (d, 0, 0, 0), **once),
        ],
        out_specs=pl.BlockSpec((1, rows, W), lambda d, i: (d, order(d, i), 0)),
        out_shape=jax.ShapeDtypeStruct((2, T, W), F32),
        scratch_shapes=[pltpu.VMEM((ncol // LANES, rows * npack, LANES), F32),
                        pltpu.VMEM((2, npack, ncol // 2), F32)],
        compiler_params=_cparams(("arbitrary", "arbitrary")),
        name="s5_scan",
    )(proj, bpack, cpack, apack)


def s5_params(a_re, a_im, log_dt, b_re, b_im, c_re, c_im, *, cfg):
    H = cfg.ssm_group
    P = cfg.ssm_state
    Gn = cfg.ssm_width // H
    gpp = MXU_DIM // H
    npack = Gn // gpp
    lam = lax.complex(a_re.astype(F32), a_im.astype(F32))
    dt = jnp.exp(log_dt.astype(F32))[..., None]
    lam_bar = jnp.exp(lam * dt)
    bmat = lax.complex(b_re.astype(F32), b_im.astype(F32))
    b_bar = ((lam_bar - 1.0) / lam)[..., None] * bmat
    eye = jnp.eye(gpp, dtype=F32)

    def blockdiag(m):
        z, n, g, r, c = m.shape
        return jnp.einsum('zngrc,gk->zngrkc', m, eye).reshape(z, n, g * r, g * c)

    bt = jnp.swapaxes(b_bar, -1, -2).reshape(2, npack, gpp, H, P)
    bpack = jnp.concatenate([blockdiag(jnp.real(bt)), blockdiag(jnp.imag(bt))], axis=-1)
    ct = jnp.swapaxes(lax.complex(c_re.astype(F32), c_im.astype(F32)), -1, -2)
    ct = ct.reshape(2, npack, gpp, P, H)
    cpack = jnp.concatenate([blockdiag(jnp.real(ct)), blockdiag(-jnp.imag(ct))], axis=-2)
    lb = lam_bar.reshape(2, 1, npack, gpp * P)
    apack = jnp.concatenate([jnp.real(lb), jnp.imag(lb)], axis=1)
    return bpack.astype(BF16), cpack.astype(BF16), apack.astype(F32)


def _ssm_post_kernel(u_ref, y_ref, d_ref, wg_ref, bg_ref, o_ref):
    y = d_ref[...] * u_ref[...].astype(F32) + y_ref[0] + y_ref[1]
    y = _gelu_exact(y)
    z = jnp.dot(y.astype(BF16), wg_ref[...], preferred_element_type=F32) + bg_ref[...]
    o_ref[...] = (y * _sigmoid(z)).astype(o_ref.dtype)


def ssm_post(proj, y2, d_skip, w_glu, b_glu, *, cfg, u_blk):
    T = proj.shape[0]
    W = cfg.ssm_width
    tm = min(cfg.tm, T)
    return pl.pallas_call(
        _ssm_post_kernel,
        grid=(T // tm,),
        in_specs=[
            pl.BlockSpec((tm, W), lambda i: (i, u_blk)),
            pl.BlockSpec((2, tm, W), lambda i: (0, i, 0)),
            pl.BlockSpec((1, W), lambda i: (0, 0)),
            pl.BlockSpec((W, W), lambda i: (0, 0)),
            pl.BlockSpec((1, W), lambda i: (0, 0)),
        ],
        out_specs=pl.BlockSpec((tm, W), lambda i: (i, 0)),
        out_shape=jax.ShapeDtypeStruct((T, W), BF16),
        compiler_params=_cparams(("arbitrary",)),
        name="ssm_post",
    )(proj, y2, d_skip.reshape(1, W).astype(F32), w_glu, b_glu.reshape(1, W).astype(F32))


def _mix_kernel(a_ref, s_ref, wa_ref, ws_ref, ga_ref, gs_ref, o_ref):
    ya = jnp.dot(a_ref[...], wa_ref[...], preferred_element_type=F32)
    ys = jnp.dot(s_ref[...], ws_ref[...], preferred_element_type=F32)
    o_ref[...] = (_sigmoid(ga_ref[...].astype(F32)) * ya
                  + _sigmoid(gs_ref[...].astype(F32)) * ys).astype(o_ref.dtype)


def gated_mix(attn, ssm, wa, ws, proj, *, cfg, ga_off, gs_off):
    T, D = attn.shape[0], cfg.d_model
    tm = min(cfg.tm, T)
    tn = min(cfg.tn, D)
    return pl.pallas_call(
        _mix_kernel,
        grid=(T // tm, D // tn),
        in_specs=[
            pl.BlockSpec((tm, attn.shape[1]), lambda i, j: (i, 0)),
            pl.BlockSpec((tm, ssm.shape[1]), lambda i, j: (i, 0)),
            pl.BlockSpec((wa.shape[0], tn), lambda i, j: (0, j)),
            pl.BlockSpec((ws.shape[0], tn), lambda i, j: (0, j)),
            pl.BlockSpec((tm, tn), lambda i, j: (i, ga_off // tn + j)),
            pl.BlockSpec((tm, tn), lambda i, j: (i, gs_off // tn + j)),
        ],
        out_specs=pl.BlockSpec((tm, tn), lambda i, j: (i, j)),
        out_shape=jax.ShapeDtypeStruct((T, D), BF16),
        compiler_params=_cparams(("arbitrary", "arbitrary")),
        name="gated_mix",
    )(attn, ssm, wa, ws, proj, proj)


def _out_proj_kernel(*refs, ranges):
    nx = len(ranges)
    m_ref, w_ref, o_ref = refs[nx:]
    i = pl.program_id(0)
    x = refs[0][...]
    for k in range(1, nx):
        x = jnp.where(i >= ranges[k][0], refs[k][...], x)
    o_ref[...] = x + jnp.dot(m_ref[...], w_ref[...], preferred_element_type=F32)


def out_proj(xs, mixed, w, *, cfg):
    T, D = mixed.shape
    tm = min(cfg.tm, min(x.shape[0] for x in xs))
    tn = min(cfg.tn, D)
    ncol = D // tn
    ranges, nrow = _row_block_ranges(xs, tm)

    def x_spec(start, nblk):
        def index(i, j):
            inside = jnp.logical_and(i >= start, i < start + nblk)
            edge = jnp.where(i < start, 0, ncol - 1)
            return jnp.clip(i - start, 0, nblk - 1), jnp.where(inside, j, edge)
        return pl.BlockSpec((tm, tn), index)

    return pl.pallas_call(
        functools.partial(_out_proj_kernel, ranges=ranges),
        grid=(nrow, ncol),
        in_specs=[x_spec(s, n) for s, n in ranges] + [
            pl.BlockSpec((tm, D), lambda i, j: (i, 0)),
            pl.BlockSpec((D, tn), lambda i, j: (0, j)),
        ],
        out_specs=pl.BlockSpec((tm, tn), lambda i, j: (i, j)),
        out_shape=jax.ShapeDtypeStruct((T, D), F32),
        compiler_params=_cparams(("arbitrary", "arbitrary")),
        name="out_proj",
    )(*xs, mixed, w)


def _topk_cols(s, k, payload=None):
    n = s.shape[0]
    row = lax.broadcasted_iota(jnp.int32, s.shape, 0).astype(F32)
    vals, picks = [], []
    for _ in range(k):
        m = jnp.max(s, axis=0, keepdims=True)
        am = jnp.min(jnp.where(s == m, row, float(n)), axis=0, keepdims=True)
        hit = row == am
        vals.append(m)
        if payload is None:
            picks.append(am)
        else:
            picks.append(jnp.sum(jnp.where(hit, payload, 0.0), axis=0, keepdims=True))
        s = jnp.where(hit, -jnp.inf, s)
    return jnp.concatenate(vals, axis=0), jnp.concatenate(picks, axis=0)


def _peer_topk_kernel(q_ref, keys_ref, idx_ref, gate_ref, *, cfg):
    K = cfg.peer_topk
    nk = cfg.peer_keys
    half = cfg.peer_qdim // 2
    for hd in range(cfg.peer_heads):
        tops = []
        for z in range(2):
            c0 = (hd * 2 + z) * half
            q = q_ref[:, c0:c0 + half]
            s = lax.dot_general(keys_ref[z], q, (((1,), (1,)), ((), ())),
                                preferred_element_type=F32,
                                precision=lax.Precision.HIGHEST)
            tops.append(_topk_cols(s, K))
        (s1, i1), (s2, i2) = tops
        cand = jnp.concatenate([s1[a:a + 1, :] + s2 for a in range(K)], axis=0)
        cidx = jnp.concatenate([i1[a:a + 1, :] * float(nk) + i2 for a in range(K)], axis=0)
        top_s, sel = _topk_cols(cand, K, payload=cidx)
        e = jnp.exp(top_s - top_s[0:1, :])
        gate = e / jnp.sum(e, axis=0, keepdims=True)
        idx_ref[hd * K:(hd + 1) * K, :] = sel.astype(jnp.int32)
        gate_ref[hd * K:(hd + 1) * K, :] = gate


def peer_topk(q, sub_keys, *, cfg):
    T = q.shape[0]
    tt = min(cfg.topk_tokens, T)
    R = cfg.peer_heads * cfg.peer_topk
    return pl.pallas_call(
        functools.partial(_peer_topk_kernel, cfg=cfg),
        grid=(T // tt,),
        in_specs=[
            pl.BlockSpec((tt, q.shape[1]), lambda i: (i, 0)),
            pl.BlockSpec(sub_keys.shape, lambda i: (0, 0, 0)),
        ],
        out_specs=[pl.BlockSpec((R, tt), lambda i: (0, i)),
                   pl.BlockSpec((R, tt), lambda i: (0, i))],
        out_shape=[jax.ShapeDtypeStruct((R, T), jnp.int32),
                   jax.ShapeDtypeStruct((R, T), F32)],
        compiler_params=_cparams(("arbitrary",)),
        name="peer_topk",
    )(q, sub_keys.astype(F32))


def pack_expert_tables(u, v):
    def pack(t):
        n, d = t.shape
        bits = lax.bitcast_convert_type(t.astype(BF16), jnp.uint16).astype(jnp.uint32)
        return bits[:, :d // 2] | (bits[:, d // 2:] << 16)
    return jnp.concatenate([pack(u), pack(v)], axis=1)[:, None, :]


def _peer_eval_kernel(idx_cur_ref, idx_nxt_ref, gate_ref, h_ref, x_ref, g_ref, tab_ref, o_ref,
                      buf_a, buf_b, sem, r_scr, out_scr, *, cfg, ngrid):
    nt = cfg.peer_tokens
    R = cfg.peer_heads * cfg.peer_topk
    NR = nt * R
    D = cfg.d_model
    Dh = D // 2
    i = pl.program_id(0)

    def issue(idx_ref, half, tok, buf, s):
        for r in range(R):
            e = idx_ref[0, 0, half * NR + tok * R + r]
            pltpu.make_async_copy(tab_ref.at[e], buf.at[pl.ds(tok * R + r, 1), :], sem.at[s]).start()

    def wait_all(buf, s):
        pltpu.make_async_copy(buf, buf, sem.at[s]).wait()

    @pl.when(i == 0)
    def _():
        for tok in range(nt):
            issue(idx_cur_ref, 0, tok, buf_a, 0)

    gate_t = gate_ref[...].T
    lane = lax.broadcasted_iota(jnp.int32, (R, LANES), 1)
    himask = jnp.uint32(0xFFFF0000)
    hl = LANES // 2

    def half_step(half, buf, s, nxt_idx_ref, nxt_half, nxt_buf, nxt_s):
        wait_all(buf, s)
        for tok in range(nt):
            issue(nxt_idx_ref, nxt_half, tok, nxt_buf, nxt_s)
            row = half * nt + tok
            base = tok * R
            view = pltpu.bitcast(buf[base:base + R, 0:Dh], BF16)
            h = h_ref[row:row + 1, :].astype(BF16)
            hcat = jnp.concatenate([jnp.broadcast_to(h[:, 0:Dh], (hl, Dh)),
                                    jnp.broadcast_to(h[:, Dh:D], (hl, Dh))], axis=0)
            r_scr[...] = lax.dot_general(view, hcat, (((1,), (1,)), ((), ())),
                                         preferred_element_type=F32)
            ra = r_scr[pl.ds(0, R, stride=2), :]
            rb = r_scr[pl.ds(1, R, stride=2), :]
            ah = ra + pltpu.roll(rb, hl, axis=1)
            act = jnp.where(lane < hl, ah, pltpu.roll(ah, hl, axis=1))
            coef = jnp.broadcast_to(gate_t[:, row:row + 1], (R, LANES)) * _gelu_exact(act)
            for c in range(Dh // LANES):
                acc_lo = jnp.zeros((SUBLANES, LANES), F32)
                acc_hi = jnp.zeros((SUBLANES, LANES), F32)
                for pr in range(R // SUBLANES):
                    w = buf[base + pr * SUBLANES:base + (pr + 1) * SUBLANES,
                            Dh + c * LANES:Dh + (c + 1) * LANES]
                    cf = coef[pr * SUBLANES:(pr + 1) * SUBLANES, :]
                    acc_lo = acc_lo + pltpu.bitcast(w << 16, F32) * cf
                    acc_hi = acc_hi + pltpu.bitcast(w & himask, F32) * cf
                out_scr[row:row + 1, c * LANES:(c + 1) * LANES] = jnp.sum(acc_lo, axis=0, keepdims=True)
                out_scr[row:row + 1, Dh + c * LANES:Dh + (c + 1) * LANES] = jnp.sum(acc_hi, axis=0, keepdims=True)

    half_step(0, buf_a, 0, idx_cur_ref, 1, buf_b, 1)
    half_step(1, buf_b, 1, idx_nxt_ref, 0, buf_a, 0)

    x2 = x_ref[...] + out_scr[...]
    ms = jnp.mean(x2 * x2, axis=-1, keepdims=True)
    o_ref[...] = x2 * lax.rsqrt(ms + cfg.eps) * g_ref[...]

    @pl.when(i == ngrid - 1)
    def _():
        wait_all(buf_a, 0)


def peer_eval(idx, gate, h2, x1, final_norm, table, *, cfg):
    T, D = x1.shape
    nt = cfg.peer_tokens
    R = cfg.peer_heads * cfg.peer_topk
    ngrid = T // (2 * nt)
    idx3 = idx.reshape(ngrid, 1, 2 * nt * R)
    return pl.pallas_call(
        functools.partial(_peer_eval_kernel, cfg=cfg, ngrid=ngrid),
        grid=(ngrid,),
        in_specs=[
            pl.BlockSpec((1, 1, 2 * nt * R), lambda i: (i, 0, 0), memory_space=pltpu.SMEM),
            pl.BlockSpec((1, 1, 2 * nt * R), lambda i: (jnp.minimum(i + 1, ngrid - 1), 0, 0),
                         memory_space=pltpu.SMEM),
            pl.BlockSpec((2 * nt, R), lambda i: (i, 0)),
            pl.BlockSpec((2 * nt, D), lambda i: (i, 0)),
            pl.BlockSpec((2 * nt, D), lambda i: (i, 0)),
            pl.BlockSpec((1, D), lambda i: (0, 0)),
            pl.BlockSpec(memory_space=pl.ANY),
        ],
        out_specs=pl.BlockSpec((2 * nt, D), lambda i: (i, 0)),
        out_shape=jax.ShapeDtypeStruct((T, D), F32),
        scratch_shapes=[
            pltpu.VMEM((nt * R, D), jnp.uint32),
            pltpu.VMEM((nt * R, D), jnp.uint32),
            pltpu.SemaphoreType.DMA((2,)),
            pltpu.VMEM((2 * R, LANES), F32),
            pltpu.VMEM((2 * nt, D), F32),
        ],
        compiler_params=_cparams(("arbitrary",)),
        name="peer_eval",
    )(idx3, idx3, gate, h2, x1, final_norm.reshape(1, D).astype(F32), table)


def encoder_layer(xs, p, *, cfg):
    aw, kw, sw, D = cfg.attn_width, cfg.kv_width, cfg.ssm_width, cfg.d_model
    w_in = p['w_in']
    src, widths = {}, {'q': aw, 'k': kw, 'v': kw, 's': sw, 'ga': D, 'gs': D}
    c = 0
    for name in ('q', 'k', 'v', 's', 'ga', 'gs'):
        src[name] = c
        c += widths[name]
    order = sorted(widths, key=lambda n: -widths[n])
    off, c = {}, 0
    for name in order:
        off[name] = c
        assert c % widths[name] == 0
        c += widths[name]
    w_perm = jnp.concatenate([w_in[:, src[n]:src[n] + widths[n]] for n in order], axis=1).astype(BF16)
    assert off['ga'] % min(cfg.tn, D) == 0 and off['gs'] % min(cfg.tn, D) == 0

    proj = norm_matmul(xs, p['norm_mix'], w_perm, cfg=cfg, out_dtype=BF16, emit_h=False)
    attn = window_attention(proj, p['attn_sink'], rope_table(cfg), cfg=cfg,
                            q_blk=off['q'] // aw, k_blk=off['k'] // kw, v_blk=off['v'] // kw)
    bpack, cpack, apack = s5_params(p['ssm_a_re'], p['ssm_a_im'], p['ssm_log_dt'], p['ssm_b_re'],
                                    p['ssm_b_im'], p['ssm_c_re'], p['ssm_c_im'], cfg=cfg)
    y2 = s5_scan(proj, bpack, cpack, apack, cfg=cfg, u_blk=off['s'] // sw)
    ssm = ssm_post(proj, y2, p['ssm_d'], p['w_glu'].astype(BF16), p['b_glu'], cfg=cfg, u_blk=off['s'] // sw)
    mixed = gated_mix(attn, ssm, p['w_attn_up'].astype(BF16), p['w_ssm_up'].astype(BF16), proj,
                      cfg=cfg, ga_off=off['ga'], gs_off=off['gs'])
    x1 = out_proj(xs, mixed, p['w_out'].astype(BF16), cfg=cfg)
    q, h2 = norm_matmul([x1], p['norm_ffn'], p['peer_w_query'].astype(BF16), cfg=cfg,
                        out_dtype=F32, emit_h=True)
    idx_t, gate_t = peer_topk(q, p['peer_sub_keys'], cfg=cfg)
    table = pack_expert_tables(p['peer_u'], p['peer_v'])
    return peer_eval(idx_t.T, gate_t.T, h2, x1, p['final_norm'], table, cfg=cfg)


_PARAM_NAMES = ('norm_mix', 'w_in', 'attn_sink', 'w_attn_up', 'ssm_a_re', 'ssm_a_im', 'ssm_log_dt',
                'ssm_b_re', 'ssm_b_im', 'ssm_c_re', 'ssm_c_im', 'ssm_d', 'w_glu', 'b_glu', 'w_ssm_up',
                'w_out', 'norm_ffn', 'peer_w_query', 'peer_sub_keys', 'peer_u', 'peer_v')


def run_layer(x_prompt, x_sample, params, final_norm, cfg):
    D = cfg.d_model
    p = {k: v[0] for k, v in zip(_PARAM_NAMES, params)}
    p['final_norm'] = final_norm
    y = encoder_layer([x_prompt.reshape(-1, D), x_sample.reshape(-1, D)], p, cfg=cfg)
    n0 = x_prompt.shape[0] * x_prompt.shape[1]
    return y[:n0].reshape(x_prompt.shape), y[n0:].reshape(x_sample.shape)


def kernel(x_prompt, x_sample, norm_mix, w_in, attn_sink, w_attn_up, ssm_a_re, ssm_a_im, ssm_log_dt, ssm_b_re, ssm_b_im, ssm_c_re, ssm_c_im, ssm_d, w_glu, b_glu, w_ssm_up, w_out, norm_ffn, peer_w_query, peer_sub_keys, peer_u, peer_v, final_norm):
    b, s, d = x_prompt.shape
    db, ds, _ = x_sample.shape
    cfg = Cfg(d_model=d, seq_lens=(s,) * b + (ds,) * db)
    params = (norm_mix, w_in, attn_sink, w_attn_up, ssm_a_re, ssm_a_im, ssm_log_dt, ssm_b_re, ssm_b_im,
              ssm_c_re, ssm_c_im, ssm_d, w_glu, b_glu, w_ssm_up, w_out, norm_ffn, peer_w_query,
              peer_sub_keys, peer_u, peer_v)
    return run_layer(x_prompt, x_sample, params, final_norm, cfg)
```

```python
import functools
import math
from typing import NamedTuple, Tuple

import jax
import jax.numpy as jnp
from jax import lax
from jax.experimental import pallas as pl
from jax.experimental.pallas import tpu as pltpu

F32 = jnp.float32
BF16 = jnp.bfloat16

LANES = 128
SUBLANES = 8
MXU_DIM = 256
VMEM_LIMIT_BYTES = 56 * 1024 * 1024


class Cfg(NamedTuple):
    d_model: int = 4096
    seq_lens: Tuple[int, ...] = (8192, 4096, 4096)
    head_dim: int = 128
    n_q: int = 16
    n_kv: int = 4
    window: int = 128
    rope_dim: int = 32
    rope_theta: float = 500000.0
    ssm_width: int = 2048
    ssm_group: int = 16
    ssm_state: int = 64
    peer_heads: int = 8
    peer_keys: int = 128
    peer_topk: int = 16
    peer_qdim: int = 256
    eps: float = 1e-6
    neg: float = -1e30
    tm: int = 512
    tn: int = 1024
    scan_rows: int = 256
    topk_tokens: int = 256
    peer_tokens: int = 8

    @property
    def tokens(self):
        return sum(self.seq_lens)

    @property
    def attn_width(self):
        return self.n_q * self.head_dim

    @property
    def kv_width(self):
        return self.n_kv * self.head_dim


def _cparams(sem):
    return pltpu.CompilerParams(dimension_semantics=sem, vmem_limit_bytes=VMEM_LIMIT_BYTES)


def _sigmoid(x):
    return 1.0 / (1.0 + jnp.exp(-x))


def _gelu_exact(x):
    return 0.5 * x * (1.0 + lax.erf(x * (2.0 ** -0.5)))


def _seq_block_flags(blk, seq_lens, rows):
    first = None
    last = None
    start = 0
    for n in seq_lens:
        f = blk == (start // rows)
        l = blk == ((start + n) // rows - 1)
        first = f if first is None else jnp.logical_or(first, f)
        last = l if last is None else jnp.logical_or(last, l)
        start += n
    return first, last


def _row_block_ranges(xs, tm):
    out, start = [], 0
    for x in xs:
        assert x.shape[0] % tm == 0
        out.append((start, x.shape[0] // tm))
        start += x.shape[0] // tm
    return out, start


def _norm_matmul_kernel(*refs, eps, emit_h, ranges, tm):
    nx = len(ranges)
    x_hbm = refs[:nx]
    g_ref, w_ref = refs[nx:nx + 2]
    if emit_h:
        o_ref, h_out_ref, x_scr, h_scr, sem = refs[nx + 2:]
    else:
        o_ref, x_scr, h_scr, sem = refs[nx + 2:]
    i = pl.program_id(0)
    nrow = pl.num_programs(0)
    slot = i % 2

    def fetch(blk, dst_slot):
        for k, (start, nblk) in enumerate(ranges):
            @pl.when(jnp.logical_and(blk >= start, blk < start + nblk))
            def _():
                pltpu.make_async_copy(x_hbm[k].at[pl.ds((blk - start) * tm, tm), :],
                                      x_scr.at[dst_slot], sem.at[dst_slot]).start()

    @pl.when(pl.program_id(1) == 0)
    def _():
        @pl.when(i == 0)
        def _():
            fetch(i, slot)

        pltpu.make_async_copy(x_scr.at[slot], x_scr.at[slot], sem.at[slot]).wait()

        @pl.when(i + 1 < nrow)
        def _():
            fetch(i + 1, 1 - slot)

        x = x_scr[slot]
        ms = jnp.mean(x * x, axis=-1, keepdims=True)
        h = x * lax.rsqrt(ms + eps) * g_ref[...]
        h_scr[...] = h.astype(BF16)
        if emit_h:
            h_out_ref[...] = h

    o_ref[...] = jnp.dot(h_scr[...], w_ref[...], preferred_element_type=F32).astype(o_ref.dtype)


def norm_matmul(xs, g, w, *, cfg, out_dtype, emit_h):
    D = xs[0].shape[1]
    T = sum(x.shape[0] for x in xs)
    N = w.shape[1]
    tm = min(cfg.tm // 2 if emit_h else cfg.tm, min(x.shape[0] for x in xs))
    tn = min(cfg.tn, N)
    assert N % tn == 0
    ranges, nrow = _row_block_ranges(xs, tm)
    out_shape = [jax.ShapeDtypeStruct((T, N), out_dtype)]
    out_specs = [pl.BlockSpec((tm, tn), lambda i, j: (i, j))]
    if emit_h:
        out_shape.append(jax.ShapeDtypeStruct((T, D), F32))
        out_specs.append(pl.BlockSpec((tm, D), lambda i, j: (i, 0)))
    res = pl.pallas_call(
        functools.partial(_norm_matmul_kernel, eps=cfg.eps, emit_h=emit_h, ranges=ranges, tm=tm),
        grid=(nrow, N // tn),
        in_specs=[pl.BlockSpec(memory_space=pl.ANY)] * len(xs) + [
            pl.BlockSpec((1, D), lambda i, j: (0, 0)),
            pl.BlockSpec((D, tn), lambda i, j: (0, j)),
        ],
        out_specs=out_specs,
        out_shape=out_shape,
        scratch_shapes=[pltpu.VMEM((2, tm, D), F32), pltpu.VMEM((tm, D), BF16), pltpu.SemaphoreType.DMA((2,))],
        compiler_params=_cparams(("arbitrary", "arbitrary")),
        name="norm_matmul_h" if emit_h else "norm_matmul",
    )(*xs, g.reshape(1, D).astype(F32), w)
    return res if emit_h else res[0]


def _rope(x, tab, hd, half):
    c = tab[:, 0:hd]
    sa = tab[:, hd:2 * hd]
    sb = tab[:, 2 * hd:3 * hd]
    return x * c + pltpu.roll(x, hd - half, axis=1) * sa + pltpu.roll(x, half, axis=1) * sb


def _attn_kernel(sink_ref, q_ref, kp_ref, kc_ref, kn_ref, vp_ref, vc_ref, vn_ref,
                 tp_ref, tc_ref, tn_ref, o_ref, *, cfg):
    hd = cfg.head_dim
    half = cfg.rope_dim // 2
    blk = cfg.window
    G = cfg.n_q // cfg.n_kv
    b = pl.program_id(0)
    first, last = _seq_block_flags(b, cfg.seq_lens, blk)
    lo = jnp.where(first, blk, 0)
    hi = jnp.where(last, 2 * blk, 3 * blk)
    kj = lax.broadcasted_iota(jnp.int32, (blk, 3 * blk), 1)
    qi = lax.broadcasted_iota(jnp.int32, (blk, 3 * blk), 0)
    rel = kj - blk - qi
    valid = (jnp.abs(rel) <= cfg.window) & (kj >= lo) & (kj < hi)
    scale = hd ** -0.5
    tabs = (tp_ref[...], tc_ref[...], tn_ref[...])
    tq = tabs[1]
    for j in range(cfg.n_kv):
        cs = slice(j * hd, (j + 1) * hd)
        ks = [_rope(r[:, cs].astype(F32), t, hd, half).astype(BF16)
              for r, t in zip((kp_ref, kc_ref, kn_ref), tabs)]
        kn = jnp.concatenate(ks, axis=0)
        vn = jnp.concatenate([vp_ref[:, cs], vc_ref[:, cs], vn_ref[:, cs]], axis=0)
        for g in range(G):
            h = j * G + g
            hs = slice(h * hd, (h + 1) * hd)
            qh = _rope(q_ref[:, hs].astype(F32), tq, hd, half).astype(BF16)
            s = lax.dot_general(qh, kn, (((1,), (1,)), ((), ())),
                                preferred_element_type=F32) * scale
            s = jnp.where(valid, s, cfg.neg)
            sink = sink_ref[h]
            m = jnp.maximum(jnp.max(s, axis=-1, keepdims=True), sink)
            p = jnp.exp(s - m)
            den = jnp.sum(p, axis=-1, keepdims=True) + jnp.exp(sink - m)
            p = p / den
            o = jnp.dot(p.astype(BF16), vn, preferred_element_type=F32)
            o_ref[:, hs] = o.astype(o_ref.dtype)


def window_attention(proj, sink, rope_tab, *, cfg, q_blk, k_blk, v_blk):
    T = proj.shape[0]
    blk = cfg.window
    nb = T // blk
    hd = cfg.head_dim
    prev = lambda b: jnp.maximum(b - 1, 0)
    nxt = lambda b: jnp.minimum(b + 1, nb - 1)
    kvw = cfg.kv_width
    kspec = lambda f, c: pl.BlockSpec((blk, kvw), lambda b: (f(b), c))
    tspec = lambda f: pl.BlockSpec((blk, 3 * hd), lambda b: (f(b), 0))
    ident = lambda b: b
    return pl.pallas_call(
        functools.partial(_attn_kernel, cfg=cfg),
        grid=(nb,),
        in_specs=[
            pl.BlockSpec(memory_space=pltpu.SMEM),
            pl.BlockSpec((blk, cfg.attn_width), lambda b: (b, q_blk)),
            kspec(prev, k_blk), kspec(ident, k_blk), kspec(nxt, k_blk),
            kspec(prev, v_blk), kspec(ident, v_blk), kspec(nxt, v_blk),
            tspec(prev), tspec(ident), tspec(nxt),
        ],
        out_specs=pl.BlockSpec((blk, cfg.attn_width), lambda b: (b, 0)),
        out_shape=jax.ShapeDtypeStruct((T, cfg.attn_width), BF16),
        compiler_params=_cparams(("arbitrary",)),
        name="window_attention",
    )(sink.astype(F32), proj, proj, proj, proj, proj, proj, proj, rope_tab, rope_tab, rope_tab)


def rope_table(cfg):
    hd = cfg.head_dim
    half = cfg.rope_dim // 2
    inv = cfg.rope_theta ** (-jnp.arange(half, dtype=F32) / half)
    tabs = []
    for n in cfg.seq_lens:
        ang = jnp.arange(n).astype(F32)[:, None] * inv[None, :]
        cos, sin = jnp.cos(ang), jnp.sin(ang)
        z = jnp.zeros((n, hd - 2 * half), F32)
        zh = jnp.zeros((n, half), F32)
        c = jnp.concatenate([cos, cos, jnp.ones((n, hd - 2 * half), F32)], axis=1)
        sa = jnp.concatenate([-sin, zh, z], axis=1)
        sb = jnp.concatenate([zh, sin, z], axis=1)
        tabs.append(jnp.concatenate([c, sa, sb], axis=1))
    return jnp.concatenate(tabs, axis=0)


SCAN_UNROLL = 4


def _s5_kernel(*refs, cfg, npack, nblk, nu):
    u_refs = refs[:nu]
    b_ref, c_ref, a_ref, y_ref, bu_scr, st_scr = refs[nu:]
    rows = cfg.scan_rows
    uw = u_refs[0].shape[1]
    pk_in = uw * nu // npack

    def u_cols(j):
        k, o = divmod(j * pk_in, uw)
        return u_refs[k][:, o:o + pk_in]

    nch = bu_scr.shape[0] // 2
    ns = nch * LANES
    d = pl.program_id(0)
    i = pl.program_id(1)
    blk = jnp.where(d == 0, i, nblk - 1 - i)
    first, last = _seq_block_flags(blk, cfg.seq_lens, rows)
    reset = jnp.where(d == 0, first, last)

    @pl.when(reset)
    def _():
        st_scr[...] = jnp.zeros_like(st_scr)

    for j in range(npack):
        bu = jnp.dot(u_cols(j), b_ref[0, j], preferred_element_type=F32)
        for c in range(2 * nch):
            bu_scr[c, pl.ds(j, rows, stride=npack), :] = bu[:, c * LANES:(c + 1) * LANES]

    ar = a_ref[0, 0]
    ai = a_ref[0, 1]

    def body(tu, carry):
        xr, xi = carry
        for k in range(SCAN_UNROLL):
            t = tu * SCAN_UNROLL + k
            row = jnp.where(d == 0, t, rows - 1 - t)
            base = pl.multiple_of(row * npack, npack)
            br = jnp.concatenate([bu_scr[c, pl.ds(base, npack), :] for c in range(nch)], axis=1)
            bi = jnp.concatenate([bu_scr[nch + c, pl.ds(base, npack), :] for c in range(nch)], axis=1)
            nxr = ar * xr - ai * xi + br
            nxi = ar * xi + ai * xr + bi
            for c in range(nch):
                bu_scr[c, pl.ds(base, npack), :] = nxr[:, c * LANES:(c + 1) * LANES]
                bu_scr[nch + c, pl.ds(base, npack), :] = nxi[:, c * LANES:(c + 1) * LANES]
            xr, xi = nxr, nxi
        return xr, xi

    xr, xi = lax.fori_loop(0, rows // SCAN_UNROLL, body, (st_scr[0], st_scr[1]))
    st_scr[0] = xr
    st_scr[1] = xi
    for j in range(npack):
        xs = jnp.concatenate([bu_scr[c, pl.ds(j, rows, stride=npack), :] for c in range(2 * nch)], axis=1)
        y_ref[0, :, j * pk_in:(j + 1) * pk_in] = jnp.dot(
            xs.astype(BF16), c_ref[0, j], preferred_element_type=F32)


def _col_blocks(off, width):
    bw = math.gcd(off, width) if off else width
    return bw, [off // bw + k for k in range(width // bw)]


def s5_scan(proj, bpack, cpack, apack, *, cfg, u_off):
    T = proj.shape[0]
    W = cfg.ssm_width
    rows = cfg.scan_rows
    nblk = T // rows
    npack = bpack.shape[1]
    ncol = bpack.shape[3]
    assert npack == SUBLANES, "one pack of groups per sublane"
    uw, ublks = _col_blocks(u_off, W)
    assert uw % (W // npack) == 0
    order = lambda d, i: jnp.where(d == 0, i, nblk - 1 - i)
    once = dict(pipeline_mode=pl.Buffered(1))
    return pl.pallas_call(
        functools.partial(_s5_kernel, cfg=cfg, npack=npack, nblk=nblk, nu=len(ublks)),
        grid=(2, nblk),
        in_specs=[pl.BlockSpec((rows, uw), lambda d, i, c=c: (order(d, i), c)) for c in ublks] + [
            pl.BlockSpec((1,) + bpack.shape[1:], lambda d, i: (d, 0, 0, 0), **once),
            pl.BlockSpec((1,) + cpack.shape[1:], lambda d, i: (d, 0, 0, 0), **once),
            pl.BlockSpec((1,) + apack.shape[1:], lambda d, i: (d, 0, 0, 0), **once),
        ],
        out_specs=pl.BlockSpec((1, rows, W), lambda d, i: (d, order(d, i), 0)),
        out_shape=jax.ShapeDtypeStruct((2, T, W), F32),
        scratch_shapes=[pltpu.VMEM((ncol // LANES, rows * npack, LANES), F32),
                        pltpu.VMEM((2, npack, ncol // 2), F32)],
        compiler_params=_cparams(("arbitrary", "arbitrary")),
        name="s5_scan",
    )(*([proj] * len(ublks)), bpack, cpack, apack)


def s5_params(a_re, a_im, log_dt, b_re, b_im, c_re, c_im, *, cfg):
    H = cfg.ssm_group
    P = cfg.ssm_state
    Gn = cfg.ssm_width // H
    gpp = MXU_DIM // H
    npack = Gn // gpp
    lam = lax.complex(a_re.astype(F32), a_im.astype(F32))
    dt = jnp.exp(log_dt.astype(F32))[..., None]
    lam_bar = jnp.exp(lam * dt)
    bmat = lax.complex(b_re.astype(F32), b_im.astype(F32))
    b_bar = ((lam_bar - 1.0) / lam)[..., None] * bmat
    eye = jnp.eye(gpp, dtype=F32)

    def blockdiag(m):
        z, n, g, r, c = m.shape
        return jnp.einsum('zngrc,gk->zngrkc', m, eye).reshape(z, n, g * r, g * c)

    bt = jnp.swapaxes(b_bar, -1, -2).reshape(2, npack, gpp, H, P)
    bpack = jnp.concatenate([blockdiag(jnp.real(bt)), blockdiag(jnp.imag(bt))], axis=-1)
    ct = jnp.swapaxes(lax.complex(c_re.astype(F32), c_im.astype(F32)), -1, -2)
    ct = ct.reshape(2, npack, gpp, P, H)
    cpack = jnp.concatenate([blockdiag(jnp.real(ct)), blockdiag(-jnp.imag(ct))], axis=-2)
    lb = lam_bar.reshape(2, 1, npack, gpp * P)
    apack = jnp.concatenate([jnp.real(lb), jnp.imag(lb)], axis=1)
    return bpack.astype(BF16), cpack.astype(BF16), apack.astype(F32)


def _ssm_post_kernel(*refs, nu):
    u_refs = refs[:nu]
    y_ref, d_ref, wg_ref, bg_ref, o_ref = refs[nu:]
    u = jnp.concatenate([r[...] for r in u_refs], axis=1) if nu > 1 else u_refs[0][...]
    y = d_ref[...] * u.astype(F32) + y_ref[0] + y_ref[1]
    y = _gelu_exact(y)
    z = jnp.dot(y.astype(BF16), wg_ref[...], preferred_element_type=F32) + bg_ref[...]
    o_ref[...] = (y * _sigmoid(z)).astype(o_ref.dtype)


def ssm_post(proj, y2, d_skip, w_glu, b_glu, *, cfg, u_off):
    T = proj.shape[0]
    W = cfg.ssm_width
    tm = min(cfg.tm, T)
    uw, ublks = _col_blocks(u_off, W)
    return pl.pallas_call(
        functools.partial(_ssm_post_kernel, nu=len(ublks)),
        grid=(T // tm,),
        in_specs=[pl.BlockSpec((tm, uw), lambda i, c=c: (i, c)) for c in ublks] + [
            pl.BlockSpec((2, tm, W), lambda i: (0, i, 0)),
            pl.BlockSpec((1, W), lambda i: (0, 0)),
            pl.BlockSpec((W, W), lambda i: (0, 0)),
            pl.BlockSpec((1, W), lambda i: (0, 0)),
        ],
        out_specs=pl.BlockSpec((tm, W), lambda i: (i, 0)),
        out_shape=jax.ShapeDtypeStruct((T, W), BF16),
        compiler_params=_cparams(("arbitrary",)),
        name="ssm_post",
    )(*([proj] * len(ublks)), y2, d_skip.reshape(1, W).astype(F32), w_glu, b_glu.reshape(1, W).astype(F32))


def _mix_kernel(a_ref, s_ref, wa_ref, ws_ref, ga_ref, gs_ref, o_ref):
    ya = jnp.dot(a_ref[...], wa_ref[...], preferred_element_type=F32)
    ys = jnp.dot(s_ref[...], ws_ref[...], preferred_element_type=F32)
    o_ref[...] = (_sigmoid(ga_ref[...].astype(F32)) * ya
                  + _sigmoid(gs_ref[...].astype(F32)) * ys).astype(o_ref.dtype)


def gated_mix(attn, ssm, wa, ws, proj, *, cfg, ga_off, gs_off):
    T, D = attn.shape[0], cfg.d_model
    tm = min(cfg.tm, T)
    tn = min(cfg.tn, D)
    return pl.pallas_call(
        _mix_kernel,
        grid=(T // tm, D // tn),
        in_specs=[
            pl.BlockSpec((tm, attn.shape[1]), lambda i, j: (i, 0)),
            pl.BlockSpec((tm, ssm.shape[1]), lambda i, j: (i, 0)),
            pl.BlockSpec((wa.shape[0], tn), lambda i, j: (0, j)),
            pl.BlockSpec((ws.shape[0], tn), lambda i, j: (0, j)),
            pl.BlockSpec((tm, tn), lambda i, j: (i, ga_off // tn + j)),
            pl.BlockSpec((tm, tn), lambda i, j: (i, gs_off // tn + j)),
        ],
        out_specs=pl.BlockSpec((tm, tn), lambda i, j: (i, j)),
        out_shape=jax.ShapeDtypeStruct((T, D), BF16),
        compiler_params=_cparams(("arbitrary", "arbitrary")),
        name="gated_mix",
    )(attn, ssm, wa, ws, proj, proj)


def _out_proj_kernel(*refs, ranges):
    nx = len(ranges)
    m_ref, w_ref, o_ref = refs[nx:]
    i = pl.program_id(0)
    x = refs[0][...]
    for k in range(1, nx):
        x = jnp.where(i >= ranges[k][0], refs[k][...], x)
    o_ref[...] = x + jnp.dot(m_ref[...], w_ref[...], preferred_element_type=F32)


def out_proj(xs, mixed, w, *, cfg):
    T, D = mixed.shape
    tm = min(cfg.tm, min(x.shape[0] for x in xs))
    tn = min(cfg.tn, D)
    ncol = D // tn
    ranges, nrow = _row_block_ranges(xs, tm)

    def x_spec(start, nblk):
        def index(i, j):
            inside = jnp.logical_and(i >= start, i < start + nblk)
            edge = jnp.where(i < start, 0, ncol - 1)
            return jnp.clip(i - start, 0, nblk - 1), jnp.where(inside, j, edge)
        return pl.BlockSpec((tm, tn), index)

    return pl.pallas_call(
        functools.partial(_out_proj_kernel, ranges=ranges),
        grid=(nrow, ncol),
        in_specs=[x_spec(s, n) for s, n in ranges] + [
            pl.BlockSpec((tm, D), lambda i, j: (i, 0)),
            pl.BlockSpec((D, tn), lambda i, j: (0, j)),
        ],
        out_specs=pl.BlockSpec((tm, tn), lambda i, j: (i, j)),
        out_shape=jax.ShapeDtypeStruct((T, D), F32),
        compiler_params=_cparams(("arbitrary", "arbitrary")),
        name="out_proj",
    )(*xs, mixed, w)


def _topk_cols(s, k, payload=None):
    n = s.shape[0]
    row = lax.broadcasted_iota(jnp.int32, s.shape, 0).astype(F32)
    vals, picks = [], []
    for _ in range(k):
        m = jnp.max(s, axis=0, keepdims=True)
        am = jnp.min(jnp.where(s == m, row, float(n)), axis=0, keepdims=True)
        hit = row == am
        vals.append(m)
        if payload is None:
            picks.append(am)
        else:
            picks.append(jnp.sum(jnp.where(hit, payload, 0.0), axis=0, keepdims=True))
        s = jnp.where(hit, -jnp.inf, s)
    return jnp.concatenate(vals, axis=0), jnp.concatenate(picks, axis=0)


def _peer_topk_kernel(q_ref, keys_ref, idx_ref, gate_ref, *, cfg):
    K = cfg.peer_topk
    nk = cfg.peer_keys
    half = cfg.peer_qdim // 2
    for hd in range(cfg.peer_heads):
        tops = []
        for z in range(2):
            c0 = (hd * 2 + z) * half
            q = q_ref[:, c0:c0 + half]
            s = lax.dot_general(keys_ref[z], q, (((1,), (1,)), ((), ())),
                                preferred_element_type=F32,
                                precision=lax.Precision.HIGHEST)
            tops.append(_topk_cols(s, K))
        (s1, i1), (s2, i2) = tops
        nb = [K // (a + 1) for a in range(K)]
        pad = -sum(nb) % SUBLANES
        cand = jnp.concatenate([s1[a:a + 1, :] + s2[0:nb[a], :] for a in range(K)]
                               + [jnp.full((pad, s1.shape[1]), -jnp.inf, F32)], axis=0)
        cidx = jnp.concatenate([i1[a:a + 1, :] * float(nk) + i2[0:nb[a], :] for a in range(K)]
                               + [jnp.zeros((pad, s1.shape[1]), F32)], axis=0)
        top_s, sel = _topk_cols(cand, K, payload=cidx)
        e = jnp.exp(top_s - top_s[0:1, :])
        gate = e / jnp.sum(e, axis=0, keepdims=True)
        idx_ref[hd * K:(hd + 1) * K, :] = sel.astype(jnp.int32)
        gate_ref[hd * K:(hd + 1) * K, :] = gate


def peer_topk(q, sub_keys, *, cfg):
    T = q.shape[0]
    tt = min(cfg.topk_tokens, T)
    R = cfg.peer_heads * cfg.peer_topk
    return pl.pallas_call(
        functools.partial(_peer_topk_kernel, cfg=cfg),
        grid=(T // tt,),
        in_specs=[
            pl.BlockSpec((tt, q.shape[1]), lambda i: (i, 0)),
            pl.BlockSpec(sub_keys.shape, lambda i: (0, 0, 0)),
        ],
        out_specs=[pl.BlockSpec((R, tt), lambda i: (0, i)),
                   pl.BlockSpec((R, tt), lambda i: (0, i))],
        out_shape=[jax.ShapeDtypeStruct((R, T), jnp.int32),
                   jax.ShapeDtypeStruct((R, T), F32)],
        compiler_params=_cparams(("arbitrary",)),
        name="peer_topk",
    )(q, sub_keys.astype(F32))


def pack_expert_tables(u, v):
    def pack(t):
        n, d = t.shape
        bits = lax.bitcast_convert_type(t.astype(BF16), jnp.uint16).astype(jnp.uint32)[:, None, :]
        return bits[:, :, :d // 2] | (bits[:, :, d // 2:] << 16)
    return jnp.concatenate([pack(u), pack(v)], axis=2)


def _peer_eval_kernel(idx_cur_ref, idx_nxt_ref, gate_ref, h_ref, x_ref, g_ref, tab_ref, o_ref,
                      buf_a, buf_b, sem, r_scr, out_scr, *, cfg, ngrid):
    nt = cfg.peer_tokens
    R = cfg.peer_heads * cfg.peer_topk
    NR = nt * R
    D = cfg.d_model
    Dh = D // 2
    i = pl.program_id(0)

    def issue(idx_ref, half, tok, buf, s):
        for r in range(R):
            e = idx_ref[0, 0, half * NR + tok * R + r]
            pltpu.make_async_copy(tab_ref.at[e], buf.at[pl.ds(tok * R + r, 1), :], sem.at[s]).start()

    def wait_all(buf, s):
        pltpu.make_async_copy(buf, buf, sem.at[s]).wait()

    @pl.when(i == 0)
    def _():
        for tok in range(nt):
            issue(idx_cur_ref, 0, tok, buf_a, 0)

    gate_t = gate_ref[...].T
    lane = lax.broadcasted_iota(jnp.int32, (R, LANES), 1)
    himask = jnp.uint32(0xFFFF0000)
    hl = LANES // 2

    def half_step(half, buf, s, nxt_idx_ref, nxt_half, nxt_buf, nxt_s):
        wait_all(buf, s)
        for tok in range(nt):
            issue(nxt_idx_ref, nxt_half, tok, nxt_buf, nxt_s)
            row = half * nt + tok
            base = tok * R
            view = pltpu.bitcast(buf[base:base + R, 0:Dh], BF16)
            h = h_ref[row:row + 1, :].astype(BF16)
            hcat = jnp.concatenate([jnp.broadcast_to(h[:, 0:Dh], (hl, Dh)),
                                    jnp.broadcast_to(h[:, Dh:D], (hl, Dh))], axis=0)
            r_scr[...] = lax.dot_general(view, hcat, (((1,), (1,)), ((), ())),
                                         preferred_element_type=F32)
            ra = r_scr[pl.ds(0, R, stride=2), :]
            rb = r_scr[pl.ds(1, R, stride=2), :]
            ah = ra + pltpu.roll(rb, hl, axis=1)
            act = jnp.where(lane < hl, ah, pltpu.roll(ah, hl, axis=1))
            coef = jnp.broadcast_to(gate_t[:, row:row + 1], (R, LANES)) * _gelu_exact(act)
            for c in range(Dh // LANES):
                acc_lo = jnp.zeros((SUBLANES, LANES), F32)
                acc_hi = jnp.zeros((SUBLANES, LANES), F32)
                for pr in range(R // SUBLANES):
                    w = buf[base + pr * SUBLANES:base + (pr + 1) * SUBLANES,
                            Dh + c * LANES:Dh + (c + 1) * LANES]
                    cf = coef[pr * SUBLANES:(pr + 1) * SUBLANES, :]
                    acc_lo = acc_lo + pltpu.bitcast(w << 16, F32) * cf
                    acc_hi = acc_hi + pltpu.bitcast(w & himask, F32) * cf
                out_scr[row:row + 1, c * LANES:(c + 1) * LANES] = jnp.sum(acc_lo, axis=0, keepdims=True)
                out_scr[row:row + 1, Dh + c * LANES:Dh + (c + 1) * LANES] = jnp.sum(acc_hi, axis=0, keepdims=True)

    half_step(0, buf_a, 0, idx_cur_ref, 1, buf_b, 1)
    half_step(1, buf_b, 1, idx_nxt_ref, 0, buf_a, 0)

    x2 = x_ref[...] + out_scr[...]
    ms = jnp.mean(x2 * x2, axis=-1, keepdims=True)
    o_ref[...] = x2 * lax.rsqrt(ms + cfg.eps) * g_ref[...]

    @pl.when(i == ngrid - 1)
    def _():
        wait_all(buf_a, 0)


def peer_eval(idx, gate, h2, x1, final_norm, table, *, cfg):
    T, D = x1.shape
    nt = cfg.peer_tokens
    R = cfg.peer_heads * cfg.peer_topk
    ngrid = T // (2 * nt)
    idx3 = idx.reshape(ngrid, 1, 2 * nt * R)
    return pl.pallas_call(
        functools.partial(_peer_eval_kernel, cfg=cfg, ngrid=ngrid),
        grid=(ngrid,),
        in_specs=[
            pl.BlockSpec((1, 1, 2 * nt * R), lambda i: (i, 0, 0), memory_space=pltpu.SMEM),
            pl.BlockSpec((1, 1, 2 * nt * R), lambda i: (jnp.minimum(i + 1, ngrid - 1), 0, 0),
                         memory_space=pltpu.SMEM),
            pl.BlockSpec((2 * nt, R), lambda i: (i, 0)),
            pl.BlockSpec((2 * nt, D), lambda i: (i, 0)),
            pl.BlockSpec((2 * nt, D), lambda i: (i, 0)),
            pl.BlockSpec((1, D), lambda i: (0, 0)),
            pl.BlockSpec(memory_space=pl.ANY),
        ],
        out_specs=pl.BlockSpec((2 * nt, D), lambda i: (i, 0)),
        out_shape=jax.ShapeDtypeStruct((T, D), F32),
        scratch_shapes=[
            pltpu.VMEM((nt * R, D), jnp.uint32),
            pltpu.VMEM((nt * R, D), jnp.uint32),
            pltpu.SemaphoreType.DMA((2,)),
            pltpu.VMEM((2 * R, LANES), F32),
            pltpu.VMEM((2 * nt, D), F32),
        ],
        compiler_params=_cparams(("arbitrary",)),
        name="peer_eval",
    )(idx3, idx3, gate, h2, x1, final_norm.reshape(1, D).astype(F32), table)


def encoder_layer(xs, p, *, cfg):
    aw, kw, sw, D = cfg.attn_width, cfg.kv_width, cfg.ssm_width, cfg.d_model
    off, c = {}, 0
    for name, width in (('q', aw), ('k', kw), ('v', kw), ('s', sw), ('ga', D), ('gs', D)):
        off[name] = c
        c += width
    assert off['q'] % aw == 0 and off['k'] % kw == 0 and off['v'] % kw == 0
    assert off['ga'] % min(cfg.tn, D) == 0 and off['gs'] % min(cfg.tn, D) == 0

    proj = norm_matmul(xs, p['norm_mix'], p['w_in'].astype(BF16), cfg=cfg, out_dtype=BF16, emit_h=False)
    attn = window_attention(proj, p['attn_sink'], rope_table(cfg), cfg=cfg,
                            q_blk=off['q'] // aw, k_blk=off['k'] // kw, v_blk=off['v'] // kw)
    bpack, cpack, apack = s5_params(p['ssm_a_re'], p['ssm_a_im'], p['ssm_log_dt'], p['ssm_b_re'],
                                    p['ssm_b_im'], p['ssm_c_re'], p['ssm_c_im'], cfg=cfg)
    y2 = s5_scan(proj, bpack, cpack, apack, cfg=cfg, u_off=off['s'])
    ssm = ssm_post(proj, y2, p['ssm_d'], p['w_glu'].astype(BF16), p['b_glu'], cfg=cfg, u_off=off['s'])
    mixed = gated_mix(attn, ssm, p['w_attn_up'].astype(BF16), p['w_ssm_up'].astype(BF16), proj,
                      cfg=cfg, ga_off=off['ga'], gs_off=off['gs'])
    x1 = out_proj(xs, mixed, p['w_out'].astype(BF16), cfg=cfg)
    q, h2 = norm_matmul([x1], p['norm_ffn'], p['peer_w_query'].astype(BF16), cfg=cfg,
                        out_dtype=F32, emit_h=True)
    idx_t, gate_t = peer_topk(q, p['peer_sub_keys'], cfg=cfg)
    table = pack_expert_tables(p['peer_u'], p['peer_v'])
    return peer_eval(idx_t.T, gate_t.T, h2, x1, p['final_norm'], table, cfg=cfg)


_PARAM_NAMES = ('norm_mix', 'w_in', 'attn_sink', 'w_attn_up', 'ssm_a_re', 'ssm_a_im', 'ssm_log_dt',
                'ssm_b_re', 'ssm_b_im', 'ssm_c_re', 'ssm_c_im', 'ssm_d', 'w_glu', 'b_glu', 'w_ssm_up',
                'w_out', 'norm_ffn', 'peer_w_query', 'peer_sub_keys', 'peer_u', 'peer_v')


def run_layer(x_prompt, x_sample, params, final_norm, cfg):
    D = cfg.d_model
    p = {k: v[0] for k, v in zip(_PARAM_NAMES, params)}
    p['final_norm'] = final_norm
    y = encoder_layer([x_prompt.reshape(-1, D), x_sample.reshape(-1, D)], p, cfg=cfg)
    n0 = x_prompt.shape[0] * x_prompt.shape[1]
    return y[:n0].reshape(x_prompt.shape), y[n0:].reshape(x_sample.shape)


def kernel(x_prompt, x_sample, norm_mix, w_in, attn_sink, w_attn_up, ssm_a_re, ssm_a_im, ssm_log_dt, ssm_b_re, ssm_b_im, ssm_c_re, ssm_c_im, ssm_d, w_glu, b_glu, w_ssm_up, w_out, norm_ffn, peer_w_query, peer_sub_keys, peer_u, peer_v, final_norm):
    b, s, d = x_prompt.shape
    db, ds, _ = x_sample.shape
    cfg = Cfg(d_model=d, seq_lens=(s,) * b + (ds,) * db)
    params = (norm_mix, w_in, attn_sink, w_attn_up, ssm_a_re, ssm_a_im, ssm_log_dt, ssm_b_re, ssm_b_im,
              ssm_c_re, ssm_c_im, ssm_d, w_glu, b_glu, w_ssm_up, w_out, norm_ffn, peer_w_query,
              peer_sub_keys, peer_u, peer_v)
    return run_layer(x_prompt, x_sample, params, final_norm, cfg)
```

```python
import functools
import math
from typing import NamedTuple, Tuple

import jax
import jax.numpy as jnp
from jax import lax
from jax.experimental import pallas as pl
from jax.experimental.pallas import tpu as pltpu

F32 = jnp.float32
BF16 = jnp.bfloat16

LANES = 128
SUBLANES = 8
MXU_DIM = 256
VMEM_LIMIT_BYTES = 56 * 1024 * 1024


class Cfg(NamedTuple):
    d_model: int = 4096
    seq_lens: Tuple[int, ...] = (8192, 4096, 4096)
    head_dim: int = 128
    n_q: int = 16
    n_kv: int = 4
    window: int = 128
    rope_dim: int = 32
    rope_theta: float = 500000.0
    ssm_width: int = 2048
    ssm_group: int = 16
    ssm_state: int = 64
    peer_heads: int = 8
    peer_keys: int = 128
    peer_topk: int = 16
    peer_qdim: int = 256
    eps: float = 1e-6
    neg: float = -1e30
    tm: int = 512
    tn: int = 1024
    scan_rows: int = 256
    topk_tokens: int = 256
    peer_tokens: int = 8

    @property
    def tokens(self):
        return sum(self.seq_lens)

    @property
    def attn_width(self):
        return self.n_q * self.head_dim

    @property
    def kv_width(self):
        return self.n_kv * self.head_dim


def _cparams(sem):
    return pltpu.CompilerParams(dimension_semantics=sem, vmem_limit_bytes=VMEM_LIMIT_BYTES)


def _sigmoid(x):
    return 1.0 / (1.0 + jnp.exp(-x))


def _gelu_exact(x):
    return 0.5 * x * (1.0 + lax.erf(x * (2.0 ** -0.5)))


def _seq_block_flags(blk, seq_lens, rows):
    first = None
    last = None
    start = 0
    for n in seq_lens:
        f = blk == (start // rows)
        l = blk == ((start + n) // rows - 1)
        first = f if first is None else jnp.logical_or(first, f)
        last = l if last is None else jnp.logical_or(last, l)
        start += n
    return first, last


def _row_block_ranges(xs, tm):
    out, start = [], 0
    for x in xs:
        assert x.shape[0] % tm == 0
        out.append((start, x.shape[0] // tm))
        start += x.shape[0] // tm
    return out, start


def _norm_matmul_kernel(*refs, eps, emit_h, ranges, tm):
    nx = len(ranges)
    x_hbm = refs[:nx]
    g_ref, w_ref = refs[nx:nx + 2]
    if emit_h:
        o_ref, h_out_ref, x_scr, h_scr, sem = refs[nx + 2:]
    else:
        o_ref, x_scr, h_scr, sem = refs[nx + 2:]
    i = pl.program_id(0)
    nrow = pl.num_programs(0)
    slot = i % 2

    def fetch(blk, dst_slot):
        for k, (start, nblk) in enumerate(ranges):
            @pl.when(jnp.logical_and(blk >= start, blk < start + nblk))
            def _():
                pltpu.make_async_copy(x_hbm[k].at[pl.ds((blk - start) * tm, tm), :],
                                      x_scr.at[dst_slot], sem.at[dst_slot]).start()

    @pl.when(pl.program_id(1) == 0)
    def _():
        @pl.when(i == 0)
        def _():
            fetch(i, slot)

        pltpu.make_async_copy(x_scr.at[slot], x_scr.at[slot], sem.at[slot]).wait()

        @pl.when(i + 1 < nrow)
        def _():
            fetch(i + 1, 1 - slot)

        x = x_scr[slot]
        ms = jnp.mean(x * x, axis=-1, keepdims=True)
        h = x * lax.rsqrt(ms + eps) * g_ref[...]
        h_scr[...] = h.astype(BF16)
        if emit_h:
            h_out_ref[...] = h

    o_ref[...] = jnp.dot(h_scr[...], w_ref[...], preferred_element_type=F32).astype(o_ref.dtype)


def norm_matmul(xs, g, w, *, cfg, out_dtype, emit_h):
    D = xs[0].shape[1]
    T = sum(x.shape[0] for x in xs)
    N = w.shape[1]
    tm = min(cfg.tm // 2 if emit_h else cfg.tm, min(x.shape[0] for x in xs))
    tn = min(cfg.tn, N)
    assert N % tn == 0
    ranges, nrow = _row_block_ranges(xs, tm)
    out_shape = [jax.ShapeDtypeStruct((T, N), out_dtype)]
    out_specs = [pl.BlockSpec((tm, tn), lambda i, j: (i, j))]
    if emit_h:
        out_shape.append(jax.ShapeDtypeStruct((T, D), F32))
        out_specs.append(pl.BlockSpec((tm, D), lambda i, j: (i, 0)))
    res = pl.pallas_call(
        functools.partial(_norm_matmul_kernel, eps=cfg.eps, emit_h=emit_h, ranges=ranges, tm=tm),
        grid=(nrow, N // tn),
        in_specs=[pl.BlockSpec(memory_space=pl.ANY)] * len(xs) + [
            pl.BlockSpec((1, D), lambda i, j: (0, 0)),
            pl.BlockSpec((D, tn), lambda i, j: (0, j)),
        ],
        out_specs=out_specs,
        out_shape=out_shape,
        scratch_shapes=[pltpu.VMEM((2, tm, D), F32), pltpu.VMEM((tm, D), BF16), pltpu.SemaphoreType.DMA((2,))],
        compiler_params=_cparams(("arbitrary", "arbitrary")),
        name="norm_matmul_h" if emit_h else "norm_matmul",
    )(*xs, g.reshape(1, D).astype(F32), w)
    return res if emit_h else res[0]


def _rope(x, tab, hd, half):
    c = tab[:, 0:hd]
    sa = tab[:, hd:2 * hd]
    sb = tab[:, 2 * hd:3 * hd]
    return x * c + pltpu.roll(x, hd - half, axis=1) * sa + pltpu.roll(x, half, axis=1) * sb


def _attn_kernel(sink_ref, q_ref, kp_ref, kc_ref, kn_ref, vp_ref, vc_ref, vn_ref,
                 tp_ref, tc_ref, tn_ref, o_ref, *, cfg):
    hd = cfg.head_dim
    half = cfg.rope_dim // 2
    blk = cfg.window
    G = cfg.n_q // cfg.n_kv
    b = pl.program_id(0)
    first, last = _seq_block_flags(b, cfg.seq_lens, blk)
    lo = jnp.where(first, blk, 0)
    hi = jnp.where(last, 2 * blk, 3 * blk)
    kj = lax.broadcasted_iota(jnp.int32, (blk, 3 * blk), 1)
    qi = lax.broadcasted_iota(jnp.int32, (blk, 3 * blk), 0)
    rel = kj - blk - qi
    valid = (jnp.abs(rel) <= cfg.window) & (kj >= lo) & (kj < hi)
    scale = hd ** -0.5
    tabs = (tp_ref[...], tc_ref[...], tn_ref[...])
    tq = tabs[1]
    for j in range(cfg.n_kv):
        cs = slice(j * hd, (j + 1) * hd)
        ks = [_rope(r[:, cs].astype(F32), t, hd, half).astype(BF16)
              for r, t in zip((kp_ref, kc_ref, kn_ref), tabs)]
        kn = jnp.concatenate(ks, axis=0)
        vn = jnp.concatenate([vp_ref[:, cs], vc_ref[:, cs], vn_ref[:, cs]], axis=0)
        for g in range(G):
            h = j * G + g
            hs = slice(h * hd, (h + 1) * hd)
            qh = (_rope(q_ref[:, hs].astype(F32), tq, hd, half) * scale).astype(BF16)
            s = lax.dot_general(qh, kn, (((1,), (1,)), ((), ())),
                                preferred_element_type=F32)
            s = jnp.where(valid, s, cfg.neg)
            sink = sink_ref[h]
            m = jnp.maximum(jnp.max(s, axis=-1, keepdims=True), sink)
            p = jnp.exp(s - m)
            den = jnp.sum(p, axis=-1, keepdims=True) + jnp.exp(sink - m)
            o = jnp.dot(p.astype(BF16), vn, preferred_element_type=F32) / den
            o_ref[:, hs] = o.astype(o_ref.dtype)


def window_attention(proj, sink, rope_tab, *, cfg, q_blk, k_blk, v_blk):
    T = proj.shape[0]
    blk = cfg.window
    nb = T // blk
    hd = cfg.head_dim
    prev = lambda b: jnp.maximum(b - 1, 0)
    nxt = lambda b: jnp.minimum(b + 1, nb - 1)
    kvw = cfg.kv_width
    kspec = lambda f, c: pl.BlockSpec((blk, kvw), lambda b: (f(b), c))
    tspec = lambda f: pl.BlockSpec((blk, 3 * hd), lambda b: (f(b), 0))
    ident = lambda b: b
    return pl.pallas_call(
        functools.partial(_attn_kernel, cfg=cfg),
        grid=(nb,),
        in_specs=[
            pl.BlockSpec(memory_space=pltpu.SMEM),
            pl.BlockSpec((blk, cfg.attn_width), lambda b: (b, q_blk)),
            kspec(prev, k_blk), kspec(ident, k_blk), kspec(nxt, k_blk),
            kspec(prev, v_blk), kspec(ident, v_blk), kspec(nxt, v_blk),
            tspec(prev), tspec(ident), tspec(nxt),
        ],
        out_specs=pl.BlockSpec((blk, cfg.attn_width), lambda b: (b, 0)),
        out_shape=jax.ShapeDtypeStruct((T, cfg.attn_width), BF16),
        compiler_params=_cparams(("arbitrary",)),
        name="window_attention",
    )(sink.astype(F32), proj, proj, proj, proj, proj, proj, proj, rope_tab, rope_tab, rope_tab)


def rope_table(cfg):
    hd = cfg.head_dim
    half = cfg.rope_dim // 2
    inv = cfg.rope_theta ** (-jnp.arange(half, dtype=F32) / half)
    tabs = []
    for n in cfg.seq_lens:
        ang = jnp.arange(n).astype(F32)[:, None] * inv[None, :]
        cos, sin = jnp.cos(ang), jnp.sin(ang)
        z = jnp.zeros((n, hd - 2 * half), F32)
        zh = jnp.zeros((n, half), F32)
        c = jnp.concatenate([cos, cos, jnp.ones((n, hd - 2 * half), F32)], axis=1)
        sa = jnp.concatenate([-sin, zh, z], axis=1)
        sb = jnp.concatenate([zh, sin, z], axis=1)
        tabs.append(jnp.concatenate([c, sa, sb], axis=1))
    return jnp.concatenate(tabs, axis=0)


SCAN_UNROLL = 4


def _s5_kernel(*refs, cfg, npack, nblk, nu):
    u_refs = refs[:nu]
    b_ref, c_ref, a_ref, y_ref, bu_scr, st_scr = refs[nu:]
    rows = cfg.scan_rows
    uw = u_refs[0].shape[1]
    pk_in = uw * nu // npack

    def u_cols(j):
        k, o = divmod(j * pk_in, uw)
        return u_refs[k][:, o:o + pk_in]

    nch = bu_scr.shape[0] // 2
    ns = nch * LANES
    d = pl.program_id(0)
    i = pl.program_id(1)
    blk = jnp.where(d == 0, i, nblk - 1 - i)
    first, last = _seq_block_flags(blk, cfg.seq_lens, rows)
    reset = jnp.where(d == 0, first, last)

    @pl.when(reset)
    def _():
        st_scr[...] = jnp.zeros_like(st_scr)

    for j in range(npack):
        bu = jnp.dot(u_cols(j), b_ref[0, j], preferred_element_type=F32)
        for c in range(2 * nch):
            bu_scr[c, pl.ds(j, rows, stride=npack), :] = bu[:, c * LANES:(c + 1) * LANES]

    ar = a_ref[0, 0]
    ai = a_ref[0, 1]

    def body(tu, carry):
        xr, xi = carry
        for k in range(SCAN_UNROLL):
            t = tu * SCAN_UNROLL + k
            row = jnp.where(d == 0, t, rows - 1 - t)
            base = pl.multiple_of(row * npack, npack)
            br = jnp.concatenate([bu_scr[c, pl.ds(base, npack), :] for c in range(nch)], axis=1)
            bi = jnp.concatenate([bu_scr[nch + c, pl.ds(base, npack), :] for c in range(nch)], axis=1)
            nxr = ar * xr - ai * xi + br
            nxi = ar * xi + ai * xr + bi
            for c in range(nch):
                bu_scr[c, pl.ds(base, npack), :] = nxr[:, c * LANES:(c + 1) * LANES]
                bu_scr[nch + c, pl.ds(base, npack), :] = nxi[:, c * LANES:(c + 1) * LANES]
            xr, xi = nxr, nxi
        return xr, xi

    xr, xi = lax.fori_loop(0, rows // SCAN_UNROLL, body, (st_scr[0], st_scr[1]))
    st_scr[0] = xr
    st_scr[1] = xi
    for j in range(npack):
        xs = jnp.concatenate([bu_scr[c, pl.ds(j, rows, stride=npack), :] for c in range(2 * nch)], axis=1)
        y_ref[0, :, j * pk_in:(j + 1) * pk_in] = jnp.dot(
            xs.astype(BF16), c_ref[0, j], preferred_element_type=F32)


def _col_blocks(off, width):
    bw = math.gcd(off, width) if off else width
    return bw, [off // bw + k for k in range(width // bw)]


def s5_scan(proj, bpack, cpack, apack, *, cfg, u_off):
    T = proj.shape[0]
    W = cfg.ssm_width
    rows = cfg.scan_rows
    nblk = T // rows
    npack = bpack.shape[1]
    ncol = bpack.shape[3]
    assert npack == SUBLANES, "one pack of groups per sublane"
    uw, ublks = _col_blocks(u_off, W)
    assert uw % (W // npack) == 0
    order = lambda d, i: jnp.where(d == 0, i, nblk - 1 - i)
    once = dict(pipeline_mode=pl.Buffered(1))
    return pl.pallas_call(
        functools.partial(_s5_kernel, cfg=cfg, npack=npack, nblk=nblk, nu=len(ublks)),
        grid=(2, nblk),
        in_specs=[pl.BlockSpec((rows, uw), lambda d, i, c=c: (order(d, i), c)) for c in ublks] + [
            pl.BlockSpec((1,) + bpack.shape[1:], lambda d, i: (d, 0, 0, 0), **once),
            pl.BlockSpec((1,) + cpack.shape[1:], lambda d, i: (d, 0, 0, 0), **once),
            pl.BlockSpec((1,) + apack.shape[1:], lambda d, i: (d, 0, 0, 0), **once),
        ],
        out_specs=pl.BlockSpec((1, rows, W), lambda d, i: (d, order(d, i), 0)),
        out_shape=jax.ShapeDtypeStruct((2, T, W), F32),
        scratch_shapes=[pltpu.VMEM((ncol // LANES, rows * npack, LANES), F32),
                        pltpu.VMEM((2, npack, ncol // 2), F32)],
        compiler_params=_cparams(("arbitrary", "arbitrary")),
        name="s5_scan",
    )(*([proj] * len(ublks)), bpack, cpack, apack)


def s5_params(a_re, a_im, log_dt, b_re, b_im, c_re, c_im, *, cfg):
    H = cfg.ssm_group
    P = cfg.ssm_state
    Gn = cfg.ssm_width // H
    gpp = MXU_DIM // H
    npack = Gn // gpp
    lam = lax.complex(a_re.astype(F32), a_im.astype(F32))
    dt = jnp.exp(log_dt.astype(F32))[..., None]
    lam_bar = jnp.exp(lam * dt)
    bmat = lax.complex(b_re.astype(F32), b_im.astype(F32))
    b_bar = ((lam_bar - 1.0) / lam)[..., None] * bmat
    eye = jnp.eye(gpp, dtype=F32)

    def blockdiag(m):
        z, n, g, r, c = m.shape
        return jnp.einsum('zngrc,gk->zngrkc', m, eye).reshape(z, n, g * r, g * c)

    bt = jnp.swapaxes(b_bar, -1, -2).reshape(2, npack, gpp, H, P)
    bpack = jnp.concatenate([blockdiag(jnp.real(bt)), blockdiag(jnp.imag(bt))], axis=-1)
    ct = jnp.swapaxes(lax.complex(c_re.astype(F32), c_im.astype(F32)), -1, -2)
    ct = ct.reshape(2, npack, gpp, P, H)
    cpack = jnp.concatenate([blockdiag(jnp.real(ct)), blockdiag(-jnp.imag(ct))], axis=-2)
    lb = lam_bar.reshape(2, 1, npack, gpp * P)
    apack = jnp.concatenate([jnp.real(lb), jnp.imag(lb)], axis=1)
    return bpack.astype(BF16), cpack.astype(BF16), apack.astype(F32)


def _ssm_post_kernel(*refs, nu):
    u_refs = refs[:nu]
    y_ref, d_ref, wg_ref, bg_ref, o_ref = refs[nu:]
    u = jnp.concatenate([r[...] for r in u_refs], axis=1) if nu > 1 else u_refs[0][...]
    y = d_ref[...] * u.astype(F32) + y_ref[0] + y_ref[1]
    y = _gelu_exact(y)
    z = jnp.dot(y.astype(BF16), wg_ref[...], preferred_element_type=F32) + bg_ref[...]
    o_ref[...] = (y * _sigmoid(z)).astype(o_ref.dtype)


def ssm_post(proj, y2, d_skip, w_glu, b_glu, *, cfg, u_off):
    T = proj.shape[0]
    W = cfg.ssm_width
    tm = min(cfg.tm, T)
    uw, ublks = _col_blocks(u_off, W)
    return pl.pallas_call(
        functools.partial(_ssm_post_kernel, nu=len(ublks)),
        grid=(T // tm,),
        in_specs=[pl.BlockSpec((tm, uw), lambda i, c=c: (i, c)) for c in ublks] + [
            pl.BlockSpec((2, tm, W), lambda i: (0, i, 0)),
            pl.BlockSpec((1, W), lambda i: (0, 0)),
            pl.BlockSpec((W, W), lambda i: (0, 0)),
            pl.BlockSpec((1, W), lambda i: (0, 0)),
        ],
        out_specs=pl.BlockSpec((tm, W), lambda i: (i, 0)),
        out_shape=jax.ShapeDtypeStruct((T, W), BF16),
        compiler_params=_cparams(("arbitrary",)),
        name="ssm_post",
    )(*([proj] * len(ublks)), y2, d_skip.reshape(1, W).astype(F32), w_glu, b_glu.reshape(1, W).astype(F32))


def _mix_kernel(a_ref, s_ref, wa_ref, ws_ref, ga_ref, gs_ref, o_ref):
    ya = jnp.dot(a_ref[...], wa_ref[...], preferred_element_type=F32)
    ys = jnp.dot(s_ref[...], ws_ref[...], preferred_element_type=F32)
    o_ref[...] = (_sigmoid(ga_ref[...].astype(F32)) * ya
                  + _sigmoid(gs_ref[...].astype(F32)) * ys).astype(o_ref.dtype)


def gated_mix(attn, ssm, wa, ws, proj, *, cfg, ga_off, gs_off):
    T, D = attn.shape[0], cfg.d_model
    tm = min(cfg.tm, T)
    tn = min(cfg.tn, D)
    return pl.pallas_call(
        _mix_kernel,
        grid=(T // tm, D // tn),
        in_specs=[
            pl.BlockSpec((tm, attn.shape[1]), lambda i, j: (i, 0)),
            pl.BlockSpec((tm, ssm.shape[1]), lambda i, j: (i, 0)),
            pl.BlockSpec((wa.shape[0], tn), lambda i, j: (0, j)),
            pl.BlockSpec((ws.shape[0], tn), lambda i, j: (0, j)),
            pl.BlockSpec((tm, tn), lambda i, j: (i, ga_off // tn + j)),
            pl.BlockSpec((tm, tn), lambda i, j: (i, gs_off // tn + j)),
        ],
        out_specs=pl.BlockSpec((tm, tn), lambda i, j: (i, j)),
        out_shape=jax.ShapeDtypeStruct((T, D), BF16),
        compiler_params=_cparams(("arbitrary", "arbitrary")),
        name="gated_mix",
    )(attn, ssm, wa, ws, proj, proj)


def _out_proj_kernel(*refs, ranges):
    nx = len(ranges)
    m_ref, w_ref, o_ref = refs[nx:]
    i = pl.program_id(0)
    x = refs[0][...]
    for k in range(1, nx):
        x = jnp.where(i >= ranges[k][0], refs[k][...], x)
    o_ref[...] = x + jnp.dot(m_ref[...], w_ref[...], preferred_element_type=F32)


def out_proj(xs, mixed, w, *, cfg):
    T, D = mixed.shape
    tm = min(cfg.tm, min(x.shape[0] for x in xs))
    tn = min(cfg.tn, D)
    ncol = D // tn
    ranges, nrow = _row_block_ranges(xs, tm)

    def x_spec(start, nblk):
        def index(i, j):
            inside = jnp.logical_and(i >= start, i < start + nblk)
            edge = jnp.where(i < start, 0, ncol - 1)
            return jnp.clip(i - start, 0, nblk - 1), jnp.where(inside, j, edge)
        return pl.BlockSpec((tm, tn), index)

    return pl.pallas_call(
        functools.partial(_out_proj_kernel, ranges=ranges),
        grid=(nrow, ncol),
        in_specs=[x_spec(s, n) for s, n in ranges] + [
            pl.BlockSpec((tm, D), lambda i, j: (i, 0)),
            pl.BlockSpec((D, tn), lambda i, j: (0, j)),
        ],
        out_specs=pl.BlockSpec((tm, tn), lambda i, j: (i, j)),
        out_shape=jax.ShapeDtypeStruct((T, D), F32),
        compiler_params=_cparams(("arbitrary", "arbitrary")),
        name="out_proj",
    )(*xs, mixed, w)


def _topk_cols(s, k, payload=None):
    n = s.shape[0]
    row = lax.broadcasted_iota(jnp.int32, s.shape, 0).astype(F32)
    vals, picks = [], []
    for _ in range(k):
        m = jnp.max(s, axis=0, keepdims=True)
        am = jnp.min(jnp.where(s == m, row, float(n)), axis=0, keepdims=True)
        hit = row == am
        vals.append(m)
        if payload is None:
            picks.append(am)
        else:
            picks.append(jnp.sum(jnp.where(hit, payload, 0.0), axis=0, keepdims=True))
        s = jnp.where(hit, -jnp.inf, s)
    return jnp.concatenate(vals, axis=0), jnp.concatenate(picks, axis=0)


def _peer_topk_kernel(q_ref, keys_ref, idx_ref, gate_ref, *, cfg):
    K = cfg.peer_topk
    nk = cfg.peer_keys
    half = cfg.peer_qdim // 2
    for hd in range(cfg.peer_heads):
        tops = []
        for z in range(2):
            c0 = (hd * 2 + z) * half
            q = q_ref[:, c0:c0 + half]
            s = lax.dot_general(keys_ref[z], q, (((1,), (1,)), ((), ())),
                                preferred_element_type=F32,
                                precision=lax.Precision.HIGHEST)
            tops.append(_topk_cols(s, K))
        (s1, i1), (s2, i2) = tops
        nb = [K // (a + 1) for a in range(K)]
        pad = -sum(nb) % SUBLANES
        cand = jnp.concatenate([s1[a:a + 1, :] + s2[0:nb[a], :] for a in range(K)]
                               + [jnp.full((pad, s1.shape[1]), -jnp.inf, F32)], axis=0)
        cidx = jnp.concatenate([i1[a:a + 1, :] * float(nk) + i2[0:nb[a], :] for a in range(K)]
                               + [jnp.zeros((pad, s1.shape[1]), F32)], axis=0)
        top_s, sel = _topk_cols(cand, K, payload=cidx)
        e = jnp.exp(top_s - top_s[0:1, :])
        gate = e / jnp.sum(e, axis=0, keepdims=True)
        idx_ref[hd * K:(hd + 1) * K, :] = sel.astype(jnp.int32)
        gate_ref[hd * K:(hd + 1) * K, :] = gate


def peer_topk(q, sub_keys, *, cfg):
    T = q.shape[0]
    tt = min(cfg.topk_tokens, T)
    R = cfg.peer_heads * cfg.peer_topk
    return pl.pallas_call(
        functools.partial(_peer_topk_kernel, cfg=cfg),
        grid=(T // tt,),
        in_specs=[
            pl.BlockSpec((tt, q.shape[1]), lambda i: (i, 0)),
            pl.BlockSpec(sub_keys.shape, lambda i: (0, 0, 0)),
        ],
        out_specs=[pl.BlockSpec((R, tt), lambda i: (0, i)),
                   pl.BlockSpec((R, tt), lambda i: (0, i))],
        out_shape=[jax.ShapeDtypeStruct((R, T), jnp.int32),
                   jax.ShapeDtypeStruct((R, T), F32)],
        compiler_params=_cparams(("arbitrary",)),
        name="peer_topk",
    )(q, sub_keys.astype(F32))


PACK_ROWS = 256


def _pack_kernel(u_ref, v_ref, o_hbm, scr, sem):
    rows, D = u_ref.shape
    Dh = D // 2
    i = pl.program_id(0)
    n = pl.num_programs(0)
    slot = i % 2

    def drain(s):
        pltpu.make_async_copy(scr.at[s], scr.at[s], sem.at[s]).wait()

    def pack(x):
        bits = pltpu.bitcast(x.astype(BF16).astype(F32), jnp.uint32)
        return (bits[:, 0:Dh] >> 16) | bits[:, Dh:D]

    @pl.when(i >= 2)
    def _():
        drain(slot)

    scr[slot, :, 0:Dh] = pack(u_ref[...])
    scr[slot, :, Dh:D] = pack(v_ref[...])
    for r in range(rows):
        pltpu.make_async_copy(scr.at[slot, pl.ds(r, 1), :], o_hbm.at[i * rows + r], sem.at[slot]).start()

    @pl.when(i == n - 1)
    def _():
        drain(slot)

        @pl.when(n > 1)
        def _():
            drain(1 - slot)


def pack_expert_tables(u, v):
    E, D = u.shape
    rows = min(PACK_ROWS, E)
    assert E % rows == 0
    return pl.pallas_call(
        _pack_kernel,
        grid=(E // rows,),
        in_specs=[pl.BlockSpec((rows, D), lambda i: (i, 0)), pl.BlockSpec((rows, D), lambda i: (i, 0))],
        out_specs=pl.BlockSpec(memory_space=pl.ANY),
        out_shape=jax.ShapeDtypeStruct((E, 1, D), jnp.uint32),
        scratch_shapes=[pltpu.VMEM((2, rows, D), jnp.uint32), pltpu.SemaphoreType.DMA((2,))],
        compiler_params=_cparams(("arbitrary",)),
        name="pack_expert_tables",
    )(u, v)


def _peer_eval_kernel(idx_cur_ref, idx_nxt_ref, gate_ref, h_ref, x_ref, g_ref, tab_ref, *rest, cfg, ngrid, ranges):
    o_refs = rest[:len(ranges)]
    buf_a, buf_b, sem, r_scr, out_scr = rest[len(ranges):]
    nt = cfg.peer_tokens
    R = cfg.peer_heads * cfg.peer_topk
    NR = nt * R
    D = cfg.d_model
    Dh = D // 2
    i = pl.program_id(0)

    def issue(idx_ref, half, tok, buf, s):
        for r in range(R):
            e = idx_ref[0, 0, half * NR + tok * R + r]
            pltpu.make_async_copy(tab_ref.at[e], buf.at[pl.ds(tok * R + r, 1), :], sem.at[s]).start()

    def wait_all(buf, s):
        pltpu.make_async_copy(buf, buf, sem.at[s]).wait()

    @pl.when(i == 0)
    def _():
        for tok in range(nt):
            issue(idx_cur_ref, 0, tok, buf_a, 0)

    gate_t = gate_ref[...].T
    lane = lax.broadcasted_iota(jnp.int32, (R, LANES), 1)
    himask = jnp.uint32(0xFFFF0000)
    hl = LANES // 2

    def half_step(half, buf, s, nxt_idx_ref, nxt_half, nxt_buf, nxt_s):
        wait_all(buf, s)
        for tok in range(nt):
            issue(nxt_idx_ref, nxt_half, tok, nxt_buf, nxt_s)
            row = half * nt + tok
            base = tok * R
            view = pltpu.bitcast(buf[base:base + R, 0:Dh], BF16)
            h = h_ref[row:row + 1, :].astype(BF16)
            hcat = jnp.concatenate([jnp.broadcast_to(h[:, 0:Dh], (hl, Dh)),
                                    jnp.broadcast_to(h[:, Dh:D], (hl, Dh))], axis=0)
            r_scr[...] = lax.dot_general(view, hcat, (((1,), (1,)), ((), ())),
                                         preferred_element_type=F32)
            ra = r_scr[pl.ds(0, R, stride=2), :]
            rb = r_scr[pl.ds(1, R, stride=2), :]
            ah = ra + pltpu.roll(rb, hl, axis=1)
            act = jnp.where(lane < hl, ah, pltpu.roll(ah, hl, axis=1))
            coef = jnp.broadcast_to(gate_t[:, row:row + 1], (R, LANES)) * _gelu_exact(act)
            for c in range(Dh // LANES):
                acc_lo = jnp.zeros((SUBLANES, LANES), F32)
                acc_hi = jnp.zeros((SUBLANES, LANES), F32)
                for pr in range(R // SUBLANES):
                    w = buf[base + pr * SUBLANES:base + (pr + 1) * SUBLANES,
                            Dh + c * LANES:Dh + (c + 1) * LANES]
                    cf = coef[pr * SUBLANES:(pr + 1) * SUBLANES, :]
                    acc_lo = acc_lo + pltpu.bitcast(w << 16, F32) * cf
                    acc_hi = acc_hi + pltpu.bitcast(w & himask, F32) * cf
                out_scr[row:row + 1, c * LANES:(c + 1) * LANES] = jnp.sum(acc_lo, axis=0, keepdims=True)
                out_scr[row:row + 1, Dh + c * LANES:Dh + (c + 1) * LANES] = jnp.sum(acc_hi, axis=0, keepdims=True)

    half_step(0, buf_a, 0, idx_cur_ref, 1, buf_b, 1)
    half_step(1, buf_b, 1, idx_nxt_ref, 0, buf_a, 0)

    x2 = x_ref[...] + out_scr[...]
    ms = jnp.mean(x2 * x2, axis=-1, keepdims=True)
    y = x2 * lax.rsqrt(ms + cfg.eps) * g_ref[...]
    for k, (start, nblk) in enumerate(ranges):
        @pl.when(jnp.logical_and(i >= start, i < start + nblk))
        def _():
            o_refs[k][...] = y

    @pl.when(i == ngrid - 1)
    def _():
        wait_all(buf_a, 0)


def peer_eval(idx, gate, h2, x1, final_norm, table, *, cfg, out_rows):
    T, D = x1.shape
    nt = cfg.peer_tokens
    R = cfg.peer_heads * cfg.peer_topk
    ngrid = T // (2 * nt)
    idx3 = idx.reshape(ngrid, 1, 2 * nt * R)
    ranges, start = [], 0
    for n in out_rows:
        assert n % (2 * nt) == 0
        ranges.append((start, n // (2 * nt)))
        start += n // (2 * nt)
    assert start == ngrid

    def out_spec(s, n):
        return pl.BlockSpec((2 * nt, D), lambda i: (jnp.clip(i - s, 0, n - 1), 0))

    return pl.pallas_call(
        functools.partial(_peer_eval_kernel, cfg=cfg, ngrid=ngrid, ranges=ranges),
        grid=(ngrid,),
        in_specs=[
            pl.BlockSpec((1, 1, 2 * nt * R), lambda i: (i, 0, 0), memory_space=pltpu.SMEM),
            pl.BlockSpec((1, 1, 2 * nt * R), lambda i: (jnp.minimum(i + 1, ngrid - 1), 0, 0),
                         memory_space=pltpu.SMEM),
            pl.BlockSpec((2 * nt, R), lambda i: (i, 0)),
            pl.BlockSpec((2 * nt, D), lambda i: (i, 0)),
            pl.BlockSpec((2 * nt, D), lambda i: (i, 0)),
            pl.BlockSpec((1, D), lambda i: (0, 0)),
            pl.BlockSpec(memory_space=pl.ANY),
        ],
        out_specs=[out_spec(s, n) for s, n in ranges],
        out_shape=[jax.ShapeDtypeStruct((n, D), F32) for n in out_rows],
        scratch_shapes=[
            pltpu.VMEM((nt * R, D), jnp.uint32),
            pltpu.VMEM((nt * R, D), jnp.uint32),
            pltpu.SemaphoreType.DMA((2,)),
            pltpu.VMEM((2 * R, LANES), F32),
            pltpu.VMEM((2 * nt, D), F32),
        ],
        compiler_params=_cparams(("arbitrary",)),
        name="peer_eval",
    )(idx3, idx3, gate, h2, x1, final_norm.reshape(1, D).astype(F32), table)


def encoder_layer(xs, p, *, cfg):
    aw, kw, sw, D = cfg.attn_width, cfg.kv_width, cfg.ssm_width, cfg.d_model
    off, c = {}, 0
    for name, width in (('q', aw), ('k', kw), ('v', kw), ('s', sw), ('ga', D), ('gs', D)):
        off[name] = c
        c += width
    assert off['q'] % aw == 0 and off['k'] % kw == 0 and off['v'] % kw == 0
    assert off['ga'] % min(cfg.tn, D) == 0 and off['gs'] % min(cfg.tn, D) == 0

    proj = norm_matmul(xs, p['norm_mix'], p['w_in'].astype(BF16), cfg=cfg, out_dtype=BF16, emit_h=False)
    attn = window_attention(proj, p['attn_sink'], rope_table(cfg), cfg=cfg,
                            q_blk=off['q'] // aw, k_blk=off['k'] // kw, v_blk=off['v'] // kw)
    bpack, cpack, apack = s5_params(p['ssm_a_re'], p['ssm_a_im'], p['ssm_log_dt'], p['ssm_b_re'],
                                    p['ssm_b_im'], p['ssm_c_re'], p['ssm_c_im'], cfg=cfg)
    y2 = s5_scan(proj, bpack, cpack, apack, cfg=cfg, u_off=off['s'])
    ssm = ssm_post(proj, y2, p['ssm_d'], p['w_glu'].astype(BF16), p['b_glu'], cfg=cfg, u_off=off['s'])
    mixed = gated_mix(attn, ssm, p['w_attn_up'].astype(BF16), p['w_ssm_up'].astype(BF16), proj,
                      cfg=cfg, ga_off=off['ga'], gs_off=off['gs'])
    x1 = out_proj(xs, mixed, p['w_out'].astype(BF16), cfg=cfg)
    q, h2 = norm_matmul([x1], p['norm_ffn'], p['peer_w_query'].astype(BF16), cfg=cfg,
                        out_dtype=F32, emit_h=True)
    idx_t, gate_t = peer_topk(q, p['peer_sub_keys'], cfg=cfg)
    table = pack_expert_tables(p['peer_u'], p['peer_v'])
    return peer_eval(idx_t.T, gate_t.T, h2, x1, p['final_norm'], table, cfg=cfg,
                     out_rows=[x.shape[0] for x in xs])


_PARAM_NAMES = ('norm_mix', 'w_in', 'attn_sink', 'w_attn_up', 'ssm_a_re', 'ssm_a_im', 'ssm_log_dt',
                'ssm_b_re', 'ssm_b_im', 'ssm_c_re', 'ssm_c_im', 'ssm_d', 'w_glu', 'b_glu', 'w_ssm_up',
                'w_out', 'norm_ffn', 'peer_w_query', 'peer_sub_keys', 'peer_u', 'peer_v')


def run_layer(x_prompt, x_sample, params, final_norm, cfg):
    D = cfg.d_model
    p = {k: v[0] for k, v in zip(_PARAM_NAMES, params)}
    p['final_norm'] = final_norm
    yp, ys = encoder_layer([x_prompt.reshape(-1, D), x_sample.reshape(-1, D)], p, cfg=cfg)
    return yp.reshape(x_prompt.shape), ys.reshape(x_sample.shape)


def kernel(x_prompt, x_sample, norm_mix, w_in, attn_sink, w_attn_up, ssm_a_re, ssm_a_im, ssm_log_dt, ssm_b_re, ssm_b_im, ssm_c_re, ssm_c_im, ssm_d, w_glu, b_glu, w_ssm_up, w_out, norm_ffn, peer_w_query, peer_sub_keys, peer_u, peer_v, final_norm):
    b, s, d = x_prompt.shape
    db, ds, _ = x_sample.shape
    cfg = Cfg(d_model=d, seq_lens=(s,) * b + (ds,) * db)
    params = (norm_mix, w_in, attn_sink, w_attn_up, ssm_a_re, ssm_a_im, ssm_log_dt, ssm_b_re, ssm_b_im,
              ssm_c_re, ssm_c_im, ssm_d, w_glu, b_glu, w_ssm_up, w_out, norm_ffn, peer_w_query,
              peer_sub_keys, peer_u, peer_v)
    return run_layer(x_prompt, x_sample, params, final_norm, cfg)
```

```python
import functools
import math
from typing import NamedTuple, Tuple

import jax
import jax.numpy as jnp
from jax import lax
from jax.experimental import pallas as pl
from jax.experimental.pallas import tpu as pltpu

F32 = jnp.float32
BF16 = jnp.bfloat16

LANES = 128
SUBLANES = 8
MXU_DIM = 256
VMEM_LIMIT_BYTES = 56 * 1024 * 1024


class Cfg(NamedTuple):
    d_model: int = 4096
    seq_lens: Tuple[int, ...] = (8192, 4096, 4096)
    head_dim: int = 128
    n_q: int = 16
    n_kv: int = 4
    window: int = 128
    rope_dim: int = 32
    rope_theta: float = 500000.0
    ssm_width: int = 2048
    ssm_group: int = 16
    ssm_state: int = 64
    peer_heads: int = 8
    peer_keys: int = 128
    peer_topk: int = 16
    peer_qdim: int = 256
    eps: float = 1e-6
    neg: float = -1e30
    tm: int = 512
    tn: int = 1024
    scan_rows: int = 256
    topk_tokens: int = 256
    peer_tokens: int = 8

    @property
    def tokens(self):
        return sum(self.seq_lens)

    @property
    def attn_width(self):
        return self.n_q * self.head_dim

    @property
    def kv_width(self):
        return self.n_kv * self.head_dim


def _cparams(sem):
    return pltpu.CompilerParams(dimension_semantics=sem, vmem_limit_bytes=VMEM_LIMIT_BYTES)


def _sigmoid(x):
    return 1.0 / (1.0 + jnp.exp(-x))


def _gelu_exact(x):
    return 0.5 * x * (1.0 + lax.erf(x * (2.0 ** -0.5)))


def _seq_block_flags(blk, seq_lens, rows):
    first = None
    last = None
    start = 0
    for n in seq_lens:
        f = blk == (start // rows)
        l = blk == ((start + n) // rows - 1)
        first = f if first is None else jnp.logical_or(first, f)
        last = l if last is None else jnp.logical_or(last, l)
        start += n
    return first, last


def _row_block_ranges(xs, tm):
    out, start = [], 0
    for x in xs:
        assert x.shape[0] % tm == 0
        out.append((start, x.shape[0] // tm))
        start += x.shape[0] // tm
    return out, start


def _norm_matmul_kernel(*refs, eps, emit_h, ranges, tm):
    nx = len(ranges)
    x_hbm = refs[:nx]
    g_ref, w_ref = refs[nx:nx + 2]
    if emit_h:
        o_ref, h_out_ref, x_scr, h_scr, sem = refs[nx + 2:]
    else:
        o_ref, x_scr, h_scr, sem = refs[nx + 2:]
    i = pl.program_id(0)
    nrow = pl.num_programs(0)
    slot = i % 2

    def fetch(blk, dst_slot):
        for k, (start, nblk) in enumerate(ranges):
            @pl.when(jnp.logical_and(blk >= start, blk < start + nblk))
            def _():
                pltpu.make_async_copy(x_hbm[k].at[pl.ds((blk - start) * tm, tm), :],
                                      x_scr.at[dst_slot], sem.at[dst_slot]).start()

    @pl.when(pl.program_id(1) == 0)
    def _():
        @pl.when(i == 0)
        def _():
            fetch(i, slot)

        pltpu.make_async_copy(x_scr.at[slot], x_scr.at[slot], sem.at[slot]).wait()

        @pl.when(i + 1 < nrow)
        def _():
            fetch(i + 1, 1 - slot)

        x = x_scr[slot]
        ms = jnp.mean(x * x, axis=-1, keepdims=True)
        h = x * lax.rsqrt(ms + eps) * g_ref[...]
        h_scr[...] = h.astype(BF16)
        if emit_h:
            h_out_ref[...] = h.astype(BF16)

    o_ref[...] = jnp.dot(h_scr[...], w_ref[...], preferred_element_type=F32).astype(o_ref.dtype)


def norm_matmul(xs, g, w, *, cfg, out_dtype, emit_h):
    D = xs[0].shape[1]
    T = sum(x.shape[0] for x in xs)
    N = w.shape[1]
    tm = min(cfg.tm, min(x.shape[0] for x in xs))
    tn = min(cfg.tn // 2 if emit_h else cfg.tn, N)
    assert N % tn == 0
    ranges, nrow = _row_block_ranges(xs, tm)
    out_shape = [jax.ShapeDtypeStruct((T, N), out_dtype)]
    out_specs = [pl.BlockSpec((tm, tn), lambda i, j: (i, j))]
    if emit_h:
        out_shape.append(jax.ShapeDtypeStruct((T, D), BF16))
        out_specs.append(pl.BlockSpec((tm, D), lambda i, j: (i, 0)))
    res = pl.pallas_call(
        functools.partial(_norm_matmul_kernel, eps=cfg.eps, emit_h=emit_h, ranges=ranges, tm=tm),
        grid=(nrow, N // tn),
        in_specs=[pl.BlockSpec(memory_space=pl.ANY)] * len(xs) + [
            pl.BlockSpec((1, D), lambda i, j: (0, 0)),
            pl.BlockSpec((D, tn), lambda i, j: (0, j)),
        ],
        out_specs=out_specs,
        out_shape=out_shape,
        scratch_shapes=[pltpu.VMEM((2, tm, D), F32), pltpu.VMEM((tm, D), BF16), pltpu.SemaphoreType.DMA((2,))],
        compiler_params=_cparams(("arbitrary", "arbitrary")),
        name="norm_matmul_h" if emit_h else "norm_matmul",
    )(*xs, g.reshape(1, D).astype(F32), w)
    return res if emit_h else res[0]


def _rope(x, tab, hd, half):
    c = tab[:, 0:hd]
    sa = tab[:, hd:2 * hd]
    sb = tab[:, 2 * hd:3 * hd]
    return x * c + pltpu.roll(x, hd - half, axis=1) * sa + pltpu.roll(x, half, axis=1) * sb


def _attn_kernel(sink_ref, q_ref, kp_ref, kc_ref, kn_ref, vp_ref, vc_ref, vn_ref,
                 tp_ref, tc_ref, tn_ref, o_ref, *, cfg):
    hd = cfg.head_dim
    half = cfg.rope_dim // 2
    blk = cfg.window
    G = cfg.n_q // cfg.n_kv
    b = pl.program_id(0)
    first, last = _seq_block_flags(b, cfg.seq_lens, blk)
    lo = jnp.where(first, blk, 0)
    hi = jnp.where(last, 2 * blk, 3 * blk)
    kj = lax.broadcasted_iota(jnp.int32, (blk, 3 * blk), 1)
    qi = lax.broadcasted_iota(jnp.int32, (blk, 3 * blk), 0)
    rel = kj - blk - qi
    valid = (jnp.abs(rel) <= cfg.window) & (kj >= lo) & (kj < hi)
    scale = hd ** -0.5
    tabs = (tp_ref[...], tc_ref[...], tn_ref[...])
    tq = tabs[1]
    for j in range(cfg.n_kv):
        cs = slice(j * hd, (j + 1) * hd)
        ks = [_rope(r[:, cs].astype(F32), t, hd, half).astype(BF16)
              for r, t in zip((kp_ref, kc_ref, kn_ref), tabs)]
        kn = jnp.concatenate(ks, axis=0)
        vn = jnp.concatenate([vp_ref[:, cs], vc_ref[:, cs], vn_ref[:, cs]], axis=0)
        for g in range(G):
            h = j * G + g
            hs = slice(h * hd, (h + 1) * hd)
            qh = (_rope(q_ref[:, hs].astype(F32), tq, hd, half) * scale).astype(BF16)
            s = lax.dot_general(qh, kn, (((1,), (1,)), ((), ())),
                                preferred_element_type=F32)
            s = jnp.where(valid, s, cfg.neg)
            sink = sink_ref[h]
            m = jnp.maximum(jnp.max(s, axis=-1, keepdims=True), sink)
            p = jnp.exp(s - m)
            den = jnp.sum(p, axis=-1, keepdims=True) + jnp.exp(sink - m)
            o = jnp.dot(p.astype(BF16), vn, preferred_element_type=F32) / den
            o_ref[:, hs] = o.astype(o_ref.dtype)


def window_attention(proj, sink, rope_tab, *, cfg, q_blk, k_blk, v_blk):
    T = proj.shape[0]
    blk = cfg.window
    nb = T // blk
    hd = cfg.head_dim
    prev = lambda b: jnp.maximum(b - 1, 0)
    nxt = lambda b: jnp.minimum(b + 1, nb - 1)
    kvw = cfg.kv_width
    kspec = lambda f, c: pl.BlockSpec((blk, kvw), lambda b: (f(b), c))
    tspec = lambda f: pl.BlockSpec((blk, 3 * hd), lambda b: (f(b), 0))
    ident = lambda b: b
    return pl.pallas_call(
        functools.partial(_attn_kernel, cfg=cfg),
        grid=(nb,),
        in_specs=[
            pl.BlockSpec(memory_space=pltpu.SMEM),
            pl.BlockSpec((blk, cfg.attn_width), lambda b: (b, q_blk)),
            kspec(prev, k_blk), kspec(ident, k_blk), kspec(nxt, k_blk),
            kspec(prev, v_blk), kspec(ident, v_blk), kspec(nxt, v_blk),
            tspec(prev), tspec(ident), tspec(nxt),
        ],
        out_specs=pl.BlockSpec((blk, cfg.attn_width), lambda b: (b, 0)),
        out_shape=jax.ShapeDtypeStruct((T, cfg.attn_width), BF16),
        compiler_params=_cparams(("arbitrary",)),
        name="window_attention",
    )(sink.astype(F32), proj, proj, proj, proj, proj, proj, proj, rope_tab, rope_tab, rope_tab)


def rope_table(cfg):
    hd = cfg.head_dim
    half = cfg.rope_dim // 2
    inv = cfg.rope_theta ** (-jnp.arange(half, dtype=F32) / half)
    tabs = []
    for n in cfg.seq_lens:
        ang = jnp.arange(n).astype(F32)[:, None] * inv[None, :]
        cos, sin = jnp.cos(ang), jnp.sin(ang)
        z = jnp.zeros((n, hd - 2 * half), F32)
        zh = jnp.zeros((n, half), F32)
        c = jnp.concatenate([cos, cos, jnp.ones((n, hd - 2 * half), F32)], axis=1)
        sa = jnp.concatenate([-sin, zh, z], axis=1)
        sb = jnp.concatenate([zh, sin, z], axis=1)
        tabs.append(jnp.concatenate([c, sa, sb], axis=1))
    return jnp.concatenate(tabs, axis=0)


SCAN_UNROLL = 4


def _s5_kernel(*refs, cfg, npack, nblk, nu):
    u_refs = refs[:nu]
    b_ref, c_ref, a_ref, y_ref, bu_scr, st_scr = refs[nu:]
    rows = cfg.scan_rows
    uw = u_refs[0].shape[1]
    pk_in = uw * nu // npack

    def u_cols(j):
        k, o = divmod(j * pk_in, uw)
        return u_refs[k][:, o:o + pk_in]

    nch = bu_scr.shape[0] // 2
    ns = nch * LANES
    d = pl.program_id(0)
    i = pl.program_id(1)
    blk = jnp.where(d == 0, i, nblk - 1 - i)
    first, last = _seq_block_flags(blk, cfg.seq_lens, rows)
    reset = jnp.where(d == 0, first, last)

    @pl.when(reset)
    def _():
        st_scr[...] = jnp.zeros_like(st_scr)

    for j in range(npack):
        bu = jnp.dot(u_cols(j), b_ref[0, j], preferred_element_type=F32)
        for c in range(2 * nch):
            bu_scr[c, pl.ds(j, rows, stride=npack), :] = bu[:, c * LANES:(c + 1) * LANES]

    ar = a_ref[0, 0]
    ai = a_ref[0, 1]

    def body(tu, carry):
        xr, xi = carry
        for k in range(SCAN_UNROLL):
            t = tu * SCAN_UNROLL + k
            row = jnp.where(d == 0, t, rows - 1 - t)
            base = pl.multiple_of(row * npack, npack)
            br = jnp.concatenate([bu_scr[c, pl.ds(base, npack), :] for c in range(nch)], axis=1)
            bi = jnp.concatenate([bu_scr[nch + c, pl.ds(base, npack), :] for c in range(nch)], axis=1)
            nxr = ar * xr - ai * xi + br
            nxi = ar * xi + ai * xr + bi
            for c in range(nch):
                bu_scr[c, pl.ds(base, npack), :] = nxr[:, c * LANES:(c + 1) * LANES]
                bu_scr[nch + c, pl.ds(base, npack), :] = nxi[:, c * LANES:(c + 1) * LANES]
            xr, xi = nxr, nxi
        return xr, xi

    xr, xi = lax.fori_loop(0, rows // SCAN_UNROLL, body, (st_scr[0], st_scr[1]))
    st_scr[0] = xr
    st_scr[1] = xi
    for j in range(npack):
        xs = jnp.concatenate([bu_scr[c, pl.ds(j, rows, stride=npack), :] for c in range(2 * nch)], axis=1)
        y_ref[0, :, j * pk_in:(j + 1) * pk_in] = jnp.dot(
            xs.astype(BF16), c_ref[0, j], preferred_element_type=F32)


def _col_blocks(off, width):
    bw = math.gcd(off, width) if off else width
    return bw, [off // bw + k for k in range(width // bw)]


def s5_scan(proj, bpack, cpack, apack, *, cfg, u_off):
    T = proj.shape[0]
    W = cfg.ssm_width
    rows = cfg.scan_rows
    nblk = T // rows
    npack = bpack.shape[1]
    ncol = bpack.shape[3]
    assert npack == SUBLANES, "one pack of groups per sublane"
    uw, ublks = _col_blocks(u_off, W)
    assert uw % (W // npack) == 0
    order = lambda d, i: jnp.where(d == 0, i, nblk - 1 - i)
    once = dict(pipeline_mode=pl.Buffered(1))
    return pl.pallas_call(
        functools.partial(_s5_kernel, cfg=cfg, npack=npack, nblk=nblk, nu=len(ublks)),
        grid=(2, nblk),
        in_specs=[pl.BlockSpec((rows, uw), lambda d, i, c=c: (order(d, i), c)) for c in ublks] + [
            pl.BlockSpec((1,) + bpack.shape[1:], lambda d, i: (d, 0, 0, 0), **once),
            pl.BlockSpec((1,) + cpack.shape[1:], lambda d, i: (d, 0, 0, 0), **once),
            pl.BlockSpec((1,) + apack.shape[1:], lambda d, i: (d, 0, 0, 0), **once),
        ],
        out_specs=pl.BlockSpec((1, rows, W), lambda d, i: (d, order(d, i), 0)),
        out_shape=jax.ShapeDtypeStruct((2, T, W), F32),
        scratch_shapes=[pltpu.VMEM((ncol // LANES, rows * npack, LANES), F32),
                        pltpu.VMEM((2, npack, ncol // 2), F32)],
        compiler_params=_cparams(("arbitrary", "arbitrary")),
        name="s5_scan",
    )(*([proj] * len(ublks)), bpack, cpack, apack)


def s5_params(a_re, a_im, log_dt, b_re, b_im, c_re, c_im, *, cfg):
    H = cfg.ssm_group
    P = cfg.ssm_state
    Gn = cfg.ssm_width // H
    gpp = MXU_DIM // H
    npack = Gn // gpp
    lam = lax.complex(a_re.astype(F32), a_im.astype(F32))
    dt = jnp.exp(log_dt.astype(F32))[..., None]
    lam_bar = jnp.exp(lam * dt)
    bmat = lax.complex(b_re.astype(F32), b_im.astype(F32))
    b_bar = ((lam_bar - 1.0) / lam)[..., None] * bmat
    eye = jnp.eye(gpp, dtype=F32)

    def blockdiag(m):
        z, n, g, r, c = m.shape
        return jnp.einsum('zngrc,gk->zngrkc', m, eye).reshape(z, n, g * r, g * c)

    bt = jnp.swapaxes(b_bar, -1, -2).reshape(2, npack, gpp, H, P)
    bpack = jnp.concatenate([blockdiag(jnp.real(bt)), blockdiag(jnp.imag(bt))], axis=-1)
    ct = jnp.swapaxes(lax.complex(c_re.astype(F32), c_im.astype(F32)), -1, -2)
    ct = ct.reshape(2, npack, gpp, P, H)
    cpack = jnp.concatenate([blockdiag(jnp.real(ct)), blockdiag(-jnp.imag(ct))], axis=-2)
    lb = lam_bar.reshape(2, 1, npack, gpp * P)
    apack = jnp.concatenate([jnp.real(lb), jnp.imag(lb)], axis=1)
    return bpack.astype(BF16), cpack.astype(BF16), apack.astype(F32)


def _ssm_post_kernel(*refs, nu):
    u_refs = refs[:nu]
    y_ref, d_ref, wg_ref, bg_ref, o_ref = refs[nu:]
    u = jnp.concatenate([r[...] for r in u_refs], axis=1) if nu > 1 else u_refs[0][...]
    y = d_ref[...] * u.astype(F32) + y_ref[0] + y_ref[1]
    y = _gelu_exact(y)
    z = jnp.dot(y.astype(BF16), wg_ref[...], preferred_element_type=F32) + bg_ref[...]
    o_ref[...] = (y * _sigmoid(z)).astype(o_ref.dtype)


def ssm_post(proj, y2, d_skip, w_glu, b_glu, *, cfg, u_off):
    T = proj.shape[0]
    W = cfg.ssm_width
    tm = min(cfg.tm, T)
    uw, ublks = _col_blocks(u_off, W)
    return pl.pallas_call(
        functools.partial(_ssm_post_kernel, nu=len(ublks)),
        grid=(T // tm,),
        in_specs=[pl.BlockSpec((tm, uw), lambda i, c=c: (i, c)) for c in ublks] + [
            pl.BlockSpec((2, tm, W), lambda i: (0, i, 0)),
            pl.BlockSpec((1, W), lambda i: (0, 0)),
            pl.BlockSpec((W, W), lambda i: (0, 0)),
            pl.BlockSpec((1, W), lambda i: (0, 0)),
        ],
        out_specs=pl.BlockSpec((tm, W), lambda i: (i, 0)),
        out_shape=jax.ShapeDtypeStruct((T, W), BF16),
        compiler_params=_cparams(("arbitrary",)),
        name="ssm_post",
    )(*([proj] * len(ublks)), y2, d_skip.reshape(1, W).astype(F32), w_glu, b_glu.reshape(1, W).astype(F32))


def _mix_kernel(a_ref, s_ref, wa_ref, ws_ref, ga_ref, gs_ref, o_ref):
    ya = jnp.dot(a_ref[...], wa_ref[...], preferred_element_type=F32)
    ys = jnp.dot(s_ref[...], ws_ref[...], preferred_element_type=F32)
    o_ref[...] = (_sigmoid(ga_ref[...].astype(F32)) * ya
                  + _sigmoid(gs_ref[...].astype(F32)) * ys).astype(o_ref.dtype)


def gated_mix(attn, ssm, wa, ws, proj, *, cfg, ga_off, gs_off):
    T, D = attn.shape[0], cfg.d_model
    tm = min(cfg.tm, T)
    tn = min(cfg.tn, D)
    return pl.pallas_call(
        _mix_kernel,
        grid=(T // tm, D // tn),
        in_specs=[
            pl.BlockSpec((tm, attn.shape[1]), lambda i, j: (i, 0)),
            pl.BlockSpec((tm, ssm.shape[1]), lambda i, j: (i, 0)),
            pl.BlockSpec((wa.shape[0], tn), lambda i, j: (0, j)),
            pl.BlockSpec((ws.shape[0], tn), lambda i, j: (0, j)),
            pl.BlockSpec((tm, tn), lambda i, j: (i, ga_off // tn + j)),
            pl.BlockSpec((tm, tn), lambda i, j: (i, gs_off // tn + j)),
        ],
        out_specs=pl.BlockSpec((tm, tn), lambda i, j: (i, j)),
        out_shape=jax.ShapeDtypeStruct((T, D), BF16),
        compiler_params=_cparams(("arbitrary", "arbitrary")),
        name="gated_mix",
    )(attn, ssm, wa, ws, proj, proj)


def _out_proj_kernel(*refs, ranges):
    nx = len(ranges)
    m_ref, w_ref, o_ref = refs[nx:]
    i = pl.program_id(0)
    x = refs[0][...]
    for k in range(1, nx):
        x = jnp.where(i >= ranges[k][0], refs[k][...], x)
    o_ref[...] = x + jnp.dot(m_ref[...], w_ref[...], preferred_element_type=F32)


def out_proj(xs, mixed, w, *, cfg):
    T, D = mixed.shape
    tm = min(cfg.tm, min(x.shape[0] for x in xs))
    tn = min(cfg.tn, D)
    ncol = D // tn
    ranges, nrow = _row_block_ranges(xs, tm)

    def x_spec(start, nblk):
        def index(i, j):
            inside = jnp.logical_and(i >= start, i < start + nblk)
            edge = jnp.where(i < start, 0, ncol - 1)
            return jnp.clip(i - start, 0, nblk - 1), jnp.where(inside, j, edge)
        return pl.BlockSpec((tm, tn), index)

    return pl.pallas_call(
        functools.partial(_out_proj_kernel, ranges=ranges),
        grid=(nrow, ncol),
        in_specs=[x_spec(s, n) for s, n in ranges] + [
            pl.BlockSpec((tm, D), lambda i, j: (i, 0)),
            pl.BlockSpec((D, tn), lambda i, j: (0, j)),
        ],
        out_specs=pl.BlockSpec((tm, tn), lambda i, j: (i, j)),
        out_shape=jax.ShapeDtypeStruct((T, D), F32),
        compiler_params=_cparams(("arbitrary", "arbitrary")),
        name="out_proj",
    )(*xs, mixed, w)


def _topk_cols(s, k, payload=None):
    n = s.shape[0]
    row = lax.broadcasted_iota(jnp.int32, s.shape, 0).astype(F32)
    vals, picks = [], []
    for _ in range(k):
        m = jnp.max(s, axis=0, keepdims=True)
        am = jnp.min(jnp.where(s == m, row, float(n)), axis=0, keepdims=True)
        hit = row == am
        vals.append(m)
        if payload is None:
            picks.append(am)
        else:
            picks.append(jnp.sum(jnp.where(hit, payload, 0.0), axis=0, keepdims=True))
        s = jnp.where(hit, -jnp.inf, s)
    return jnp.concatenate(vals, axis=0), jnp.concatenate(picks, axis=0)


def _peer_topk_kernel(q_ref, keys_ref, idx_ref, gate_ref, *, cfg):
    K = cfg.peer_topk
    nk = cfg.peer_keys
    half = cfg.peer_qdim // 2
    for hd in range(cfg.peer_heads):
        tops = []
        for z in range(2):
            c0 = (hd * 2 + z) * half
            q = q_ref[:, c0:c0 + half]
            s = lax.dot_general(keys_ref[z], q, (((1,), (1,)), ((), ())),
                                preferred_element_type=F32,
                                precision=lax.Precision.HIGHEST)
            tops.append(_topk_cols(s, K))
        (s1, i1), (s2, i2) = tops
        nb = [K // (a + 1) for a in range(K)]
        pad = -sum(nb) % SUBLANES
        cand = jnp.concatenate([s1[a:a + 1, :] + s2[0:nb[a], :] for a in range(K)]
                               + [jnp.full((pad, s1.shape[1]), -jnp.inf, F32)], axis=0)
        cidx = jnp.concatenate([i1[a:a + 1, :] * float(nk) + i2[0:nb[a], :] for a in range(K)]
                               + [jnp.zeros((pad, s1.shape[1]), F32)], axis=0)
        top_s, sel = _topk_cols(cand, K, payload=cidx)
        e = jnp.exp(top_s - top_s[0:1, :])
        gate = e / jnp.sum(e, axis=0, keepdims=True)
        idx_ref[hd * K:(hd + 1) * K, :] = sel.astype(jnp.int32)
        gate_ref[hd * K:(hd + 1) * K, :] = gate


def peer_topk(q, sub_keys, *, cfg):
    T = q.shape[0]
    tt = min(cfg.topk_tokens, T)
    R = cfg.peer_heads * cfg.peer_topk
    return pl.pallas_call(
        functools.partial(_peer_topk_kernel, cfg=cfg),
        grid=(T // tt,),
        in_specs=[
            pl.BlockSpec((tt, q.shape[1]), lambda i: (i, 0)),
            pl.BlockSpec(sub_keys.shape, lambda i: (0, 0, 0)),
        ],
        out_specs=[pl.BlockSpec((R, tt), lambda i: (0, i)),
                   pl.BlockSpec((R, tt), lambda i: (0, i))],
        out_shape=[jax.ShapeDtypeStruct((R, T), jnp.int32),
                   jax.ShapeDtypeStruct((R, T), F32)],
        compiler_params=_cparams(("arbitrary",)),
        name="peer_topk",
    )(q, sub_keys.astype(F32))


PACK_ROWS = 256


def _pack_kernel(u_ref, v_ref, o_hbm, scr, sem):
    rows, D = u_ref.shape
    Dh = D // 2
    i = pl.program_id(0)
    n = pl.num_programs(0)
    slot = i % 2

    def drain(s):
        pltpu.make_async_copy(scr.at[s], scr.at[s], sem.at[s]).wait()

    def pack(x):
        bits = pltpu.bitcast(x.astype(BF16).astype(F32), jnp.uint32)
        return (bits[:, 0:Dh] >> 16) | bits[:, Dh:D]

    @pl.when(i >= 2)
    def _():
        drain(slot)

    scr[slot, :, 0:Dh] = pack(u_ref[...])
    scr[slot, :, Dh:D] = pack(v_ref[...])
    for r in range(rows):
        pltpu.make_async_copy(scr.at[slot, pl.ds(r, 1), :], o_hbm.at[i * rows + r], sem.at[slot]).start()

    @pl.when(i == n - 1)
    def _():
        drain(slot)

        @pl.when(n > 1)
        def _():
            drain(1 - slot)


def pack_expert_tables(u, v):
    E, D = u.shape
    rows = min(PACK_ROWS, E)
    assert E % rows == 0
    return pl.pallas_call(
        _pack_kernel,
        grid=(E // rows,),
        in_specs=[pl.BlockSpec((rows, D), lambda i: (i, 0)), pl.BlockSpec((rows, D), lambda i: (i, 0))],
        out_specs=pl.BlockSpec(memory_space=pl.ANY),
        out_shape=jax.ShapeDtypeStruct((E, 1, D), jnp.uint32),
        scratch_shapes=[pltpu.VMEM((2, rows, D), jnp.uint32), pltpu.SemaphoreType.DMA((2,))],
        compiler_params=_cparams(("arbitrary",)),
        name="pack_expert_tables",
    )(u, v)


def _peer_eval_kernel(idx_cur_ref, idx_nxt_ref, gate_ref, h_ref, x_ref, g_ref, tab_ref, *rest, cfg, ngrid, ranges):
    o_refs = rest[:len(ranges)]
    buf_a, buf_b, sem, r_scr, out_scr = rest[len(ranges):]
    nt = cfg.peer_tokens
    R = cfg.peer_heads * cfg.peer_topk
    NR = nt * R
    D = cfg.d_model
    Dh = D // 2
    i = pl.program_id(0)

    def issue(idx_ref, half, tok, buf, s):
        for r in range(R):
            e = idx_ref[0, 0, half * NR + tok * R + r]
            pltpu.make_async_copy(tab_ref.at[e], buf.at[pl.ds(tok * R + r, 1), :], sem.at[s]).start()

    def wait_all(buf, s):
        pltpu.make_async_copy(buf, buf, sem.at[s]).wait()

    @pl.when(i == 0)
    def _():
        for tok in range(nt):
            issue(idx_cur_ref, 0, tok, buf_a, 0)

    gate_t = gate_ref[...].T
    lane = lax.broadcasted_iota(jnp.int32, (R, LANES), 1)
    himask = jnp.uint32(0xFFFF0000)
    hl = LANES // 2

    def half_step(half, buf, s, nxt_idx_ref, nxt_half, nxt_buf, nxt_s):
        wait_all(buf, s)
        for tok in range(nt):
            issue(nxt_idx_ref, nxt_half, tok, nxt_buf, nxt_s)
            row = half * nt + tok
            base = tok * R
            view = pltpu.bitcast(buf[base:base + R, 0:Dh], BF16)
            h = h_ref[row:row + 1, :]
            hcat = jnp.concatenate([jnp.broadcast_to(h[:, 0:Dh], (hl, Dh)),
                                    jnp.broadcast_to(h[:, Dh:D], (hl, Dh))], axis=0)
            r_scr[...] = lax.dot_general(view, hcat, (((1,), (1,)), ((), ())),
                                         preferred_element_type=F32)
            ra = r_scr[pl.ds(0, R, stride=2), :]
            rb = r_scr[pl.ds(1, R, stride=2), :]
            ah = ra + pltpu.roll(rb, hl, axis=1)
            act = jnp.where(lane < hl, ah, pltpu.roll(ah, hl, axis=1))
            coef = jnp.broadcast_to(gate_t[:, row:row + 1], (R, LANES)) * _gelu_exact(act)
            for c in range(Dh // LANES):
                acc_lo = jnp.zeros((SUBLANES, LANES), F32)
                acc_hi = jnp.zeros((SUBLANES, LANES), F32)
                for pr in range(R // SUBLANES):
                    w = buf[base + pr * SUBLANES:base + (pr + 1) * SUBLANES,
                            Dh + c * LANES:Dh + (c + 1) * LANES]
                    cf = coef[pr * SUBLANES:(pr + 1) * SUBLANES, :]
                    acc_lo = acc_lo + pltpu.bitcast(w << 16, F32) * cf
                    acc_hi = acc_hi + pltpu.bitcast(w & himask, F32) * cf
                out_scr[row:row + 1, c * LANES:(c + 1) * LANES] = jnp.sum(acc_lo, axis=0, keepdims=True)
                out_scr[row:row + 1, Dh + c * LANES:Dh + (c + 1) * LANES] = jnp.sum(acc_hi, axis=0, keepdims=True)

    half_step(0, buf_a, 0, idx_cur_ref, 1, buf_b, 1)
    half_step(1, buf_b, 1, idx_nxt_ref, 0, buf_a, 0)

    x2 = x_ref[...] + out_scr[...]
    ms = jnp.mean(x2 * x2, axis=-1, keepdims=True)
    y = x2 * lax.rsqrt(ms + cfg.eps) * g_ref[...]
    for k, (start, nblk) in enumerate(ranges):
        @pl.when(jnp.logical_and(i >= start, i < start + nblk))
        def _():
            o_refs[k][...] = y

    @pl.when(i == ngrid - 1)
    def _():
        wait_all(buf_a, 0)


def peer_eval(idx, gate, h2, x1, final_norm, table, *, cfg, out_rows):
    T, D = x1.shape
    nt = cfg.peer_tokens
    R = cfg.peer_heads * cfg.peer_topk
    ngrid = T // (2 * nt)
    idx3 = idx.reshape(ngrid, 1, 2 * nt * R)
    ranges, start = [], 0
    for n in out_rows:
        assert n % (2 * nt) == 0
        ranges.append((start, n // (2 * nt)))
        start += n // (2 * nt)
    assert start == ngrid

    def out_spec(s, n):
        return pl.BlockSpec((2 * nt, D), lambda i: (jnp.clip(i - s, 0, n - 1), 0))

    return pl.pallas_call(
        functools.partial(_peer_eval_kernel, cfg=cfg, ngrid=ngrid, ranges=ranges),
        grid=(ngrid,),
        in_specs=[
            pl.BlockSpec((1, 1, 2 * nt * R), lambda i: (i, 0, 0), memory_space=pltpu.SMEM),
            pl.BlockSpec((1, 1, 2 * nt * R), lambda i: (jnp.minimum(i + 1, ngrid - 1), 0, 0),
                         memory_space=pltpu.SMEM),
            pl.BlockSpec((2 * nt, R), lambda i: (i, 0)),
            pl.BlockSpec((2 * nt, D), lambda i: (i, 0)),
            pl.BlockSpec((2 * nt, D), lambda i: (i, 0)),
            pl.BlockSpec((1, D), lambda i: (0, 0)),
            pl.BlockSpec(memory_space=pl.ANY),
        ],
        out_specs=[out_spec(s, n) for s, n in ranges],
        out_shape=[jax.ShapeDtypeStruct((n, D), F32) for n in out_rows],
        scratch_shapes=[
            pltpu.VMEM((nt * R, D), jnp.uint32),
            pltpu.VMEM((nt * R, D), jnp.uint32),
            pltpu.SemaphoreType.DMA((2,)),
            pltpu.VMEM((2 * R, LANES), F32),
            pltpu.VMEM((2 * nt, D), F32),
        ],
        compiler_params=_cparams(("arbitrary",)),
        name="peer_eval",
    )(idx3, idx3, gate, h2, x1, final_norm.reshape(1, D).astype(F32), table)


def encoder_layer(xs, p, *, cfg):
    aw, kw, sw, D = cfg.attn_width, cfg.kv_width, cfg.ssm_width, cfg.d_model
    off, c = {}, 0
    for name, width in (('q', aw), ('k', kw), ('v', kw), ('s', sw), ('ga', D), ('gs', D)):
        off[name] = c
        c += width
    assert off['q'] % aw == 0 and off['k'] % kw == 0 and off['v'] % kw == 0
    assert off['ga'] % min(cfg.tn, D) == 0 and off['gs'] % min(cfg.tn, D) == 0

    proj = norm_matmul(xs, p['norm_mix'], p['w_in'].astype(BF16), cfg=cfg, out_dtype=BF16, emit_h=False)
    attn = window_attention(proj, p['attn_sink'], rope_table(cfg), cfg=cfg,
                            q_blk=off['q'] // aw, k_blk=off['k'] // kw, v_blk=off['v'] // kw)
    bpack, cpack, apack = s5_params(p['ssm_a_re'], p['ssm_a_im'], p['ssm_log_dt'], p['ssm_b_re'],
                                    p['ssm_b_im'], p['ssm_c_re'], p['ssm_c_im'], cfg=cfg)
    y2 = s5_scan(proj, bpack, cpack, apack, cfg=cfg, u_off=off['s'])
    ssm = ssm_post(proj, y2, p['ssm_d'], p['w_glu'].astype(BF16), p['b_glu'], cfg=cfg, u_off=off['s'])
    mixed = gated_mix(attn, ssm, p['w_attn_up'].astype(BF16), p['w_ssm_up'].astype(BF16), proj,
                      cfg=cfg, ga_off=off['ga'], gs_off=off['gs'])
    x1 = out_proj(xs, mixed, p['w_out'].astype(BF16), cfg=cfg)
    q, h2 = norm_matmul([x1], p['norm_ffn'], p['peer_w_query'].astype(BF16), cfg=cfg,
                        out_dtype=F32, emit_h=True)
    idx_t, gate_t = peer_topk(q, p['peer_sub_keys'], cfg=cfg)
    table = pack_expert_tables(p['peer_u'], p['peer_v'])
    return peer_eval(idx_t.T, gate_t.T, h2, x1, p['final_norm'], table, cfg=cfg,
                     out_rows=[x.shape[0] for x in xs])


_PARAM_NAMES = ('norm_mix', 'w_in', 'attn_sink', 'w_attn_up', 'ssm_a_re', 'ssm_a_im', 'ssm_log_dt',
                'ssm_b_re', 'ssm_b_im', 'ssm_c_re', 'ssm_c_im', 'ssm_d', 'w_glu', 'b_glu', 'w_ssm_up',
                'w_out', 'norm_ffn', 'peer_w_query', 'peer_sub_keys', 'peer_u', 'peer_v')


def run_layer(x_prompt, x_sample, params, final_norm, cfg):
    D = cfg.d_model
    p = {k: v[0] for k, v in zip(_PARAM_NAMES, params)}
    p['final_norm'] = final_norm
    yp, ys = encoder_layer([x_prompt.reshape(-1, D), x_sample.reshape(-1, D)], p, cfg=cfg)
    return yp.reshape(x_prompt.shape), ys.reshape(x_sample.shape)


def kernel(x_prompt, x_sample, norm_mix, w_in, attn_sink, w_attn_up, ssm_a_re, ssm_a_im, ssm_log_dt, ssm_b_re, ssm_b_im, ssm_c_re, ssm_c_im, ssm_d, w_glu, b_glu, w_ssm_up, w_out, norm_ffn, peer_w_query, peer_sub_keys, peer_u, peer_v, final_norm):
    b, s, d = x_prompt.shape
    db, ds, _ = x_sample.shape
    cfg = Cfg(d_model=d, seq_lens=(s,) * b + (ds,) * db)
    params = (norm_mix, w_in, attn_sink, w_attn_up, ssm_a_re, ssm_a_im, ssm_log_dt, ssm_b_re, ssm_b_im,
              ssm_c_re, ssm_c_im, ssm_d, w_glu, b_glu, w_ssm_up, w_out, norm_ffn, peer_w_query,
              peer_sub_keys, peer_u, peer_v)
    return run_layer(x_prompt, x_sample, params, final_norm, cfg)
```
